```python
import math
import jax, jax.numpy as jnp
from jax import lax
import numpy as np

D_MODEL = 1024
BATCH = 8
SEQ = 4096
DEPTH = 2

N_MIXERS = 2
N_NSA_LAYERS = (DEPTH + 1) // 2
N_DIFF_LAYERS = DEPTH // 2
EPS = 1e-6
ROPE_THETA = 500000.0
ROT_FRACTION_DIV = 4
D_FF = 4 * D_MODEL
NEG_INF = -1e30
BIG = 1e9

NSA_HEAD_DIM = 64
NSA_HEADS = D_MODEL // NSA_HEAD_DIM
NSA_KV_GROUPS = 4
NSA_HPG = NSA_HEADS // NSA_KV_GROUPS
CMP_BLOCK = 32
CMP_STRIDE = 16
CMP_HIDDEN = 4 * NSA_HEAD_DIM
SLC_BLOCK = 64
SLC_TOPK = 16
WINDOW = 512
NSA_Q_BLOCK = 32
NSA_KV_WIDTH = NSA_KV_GROUPS * NSA_HEAD_DIM
NSA_IN_SPLITS = [NSA_HEADS * NSA_HEAD_DIM] + [NSA_KV_WIDTH] * 6 + [NSA_HEADS * 3]
NSA_IN_WIDTH = sum(NSA_IN_SPLITS)

DIFF_HEAD_DIM = 64
DIFF_HEADS = D_MODEL // (2 * DIFF_HEAD_DIM)
DIFF_IN_WIDTH = 3 * D_MODEL
ATTN_Q_BLOCK = 128

kernel_name = "hybrid_nsa_diffattn_sqrelu"


def rms_norm(x, g):
    xf = x.astype(jnp.float32)
    y = xf * lax.rsqrt(jnp.mean(xf * xf, axis=-1, keepdims=True) + EPS)
    return (y * g.astype(jnp.float32)).astype(x.dtype)


def rope_tables(positions, head_dim):
    rot = head_dim // ROT_FRACTION_DIV
    inv = 1.0 / (ROPE_THETA ** (jnp.arange(0, rot, 2, dtype=jnp.float32) / rot))
    ang = positions.astype(jnp.float32)[:, None] * inv[None, :]
    return jnp.cos(ang), jnp.sin(ang)


def apply_partial_rope(x, cos, sin):
    half = cos.shape[-1]
    x1, x2, xp = x[..., :half], x[..., half:2 * half], x[..., 2 * half:]
    c = cos.astype(x.dtype)
    s = sin.astype(x.dtype)
    return jnp.concatenate([x1 * c - x2 * s, x2 * c + x1 * s, xp], axis=-1)


def masked_softmax(s, mask):
    s = jnp.where(mask, s.astype(jnp.float32), NEG_INF)
    p = jax.nn.softmax(s, axis=-1)
    return jnp.where(mask, p, 0.0)


def compress_blocks(kv, pos_emb, w1, w2):
    S = kv.shape[2]
    nc = (S - CMP_BLOCK) // CMP_STRIDE + 1
    idx = jnp.arange(nc)[:, None] * CMP_STRIDE + jnp.arange(CMP_BLOCK)[None, :]
    blocks = kv[:, :, idx, :] + pos_emb.astype(kv.dtype)
    flat = blocks.reshape(blocks.shape[:3] + (CMP_BLOCK * NSA_HEAD_DIM,))
    return jax.nn.gelu(flat @ w1) @ w2


def nsa_mixer(h, w_in, ck_pos, ck_w1, ck_w2, cv_pos, cv_w1, cv_w2, w_out):
    B, S, _ = h.shape
    G, P, hd = NSA_KV_GROUPS, NSA_HPG, NSA_HEAD_DIM
    proj = h @ w_in
    q, k_cmp, v_cmp, k_slc, v_slc, k_win, v_win, g_log = jnp.split(
        proj, list(np.cumsum(NSA_IN_SPLITS)[:-1]), axis=-1)
    q = q.reshape(B, S, G, P, hd).transpose(0, 2, 3, 1, 4)
    gates = jax.nn.sigmoid(g_log.reshape(B, S, G, P, 3).astype(jnp.float32))
    gates = gates.transpose(0, 2, 3, 1, 4).astype(h.dtype)
    kvg = lambda t: t.reshape(B, S, G, hd).transpose(0, 2, 1, 3)
    k_cmp, v_cmp, k_slc, v_slc, k_win, v_win = map(kvg, (k_cmp, v_cmp, k_slc, v_slc, k_win, v_win))

    pos = jnp.arange(S)
    cos, sin = rope_tables(pos, hd)
    q = apply_partial_rope(q, cos, sin)
    k_slc = apply_partial_rope(k_slc, cos, sin)
    k_win = apply_partial_rope(k_win, cos, sin)

    nc = (S - CMP_BLOCK) // CMP_STRIDE + 1
    cmp_start = jnp.arange(nc) * CMP_STRIDE
    cmp_end = cmp_start + CMP_BLOCK - 1
    kc = compress_blocks(k_cmp, ck_pos, ck_w1, ck_w2)
    vc = compress_blocks(v_cmp, cv_pos, cv_w1, cv_w2)
    ccos, csin = rope_tables(cmp_end, hd)
    kc = apply_partial_rope(kc, ccos, csin)

    n_slc = S // SLC_BLOCK
    topk = min(SLC_TOPK, n_slc)
    sj = jnp.arange(n_slc) * SLC_BLOCK
    sel_map = ((cmp_end[:, None] >= sj[None, :]) &
               (cmp_start[:, None] <= sj[None, :] + SLC_BLOCK - 1)).astype(jnp.float32)
    ks_blocks = k_slc.reshape(B, G, n_slc, SLC_BLOCK, hd)
    vs_blocks = v_slc.reshape(B, G, n_slc, SLC_BLOCK, hd)
    gather = jax.vmap(jax.vmap(lambda blk, ix: blk[ix]))

    k_win_pad = jnp.pad(k_win, ((0, 0), (0, 0), (WINDOW, 0), (0, 0)))
    v_win_pad = jnp.pad(v_win, ((0, 0), (0, 0), (WINDOW, 0), (0, 0)))

    scale = hd ** -0.5
    Cq = NSA_Q_BLOCK
    blk_ids = jnp.arange(n_slc)

    def chunk(c):
        t0 = c * Cq
        tq = t0 + jnp.arange(Cq)
        qc = lax.dynamic_slice_in_dim(q, t0, Cq, axis=3) * jnp.asarray(scale, q.dtype)
        gc = lax.dynamic_slice_in_dim(gates, t0, Cq, axis=3)

        s_c = jnp.einsum('bgpqd,bgnd->bgpqn', qc, kc)
        p_c = masked_softmax(s_c, cmp_end[None, :] <= tq[:, None])
        o_c = jnp.einsum('bgpqn,bgnd->bgpqd', p_c.astype(vc.dtype), vc)

        imp = jnp.einsum('bgpqn,nj->bgqj', p_c, sel_map)
        cur = (tq // SLC_BLOCK)[:, None]
        forced = (blk_ids[None, :] == 0) | (blk_ids[None, :] == cur) | (blk_ids[None, :] == cur - 1)
        future = blk_ids[None, :] * SLC_BLOCK > tq[:, None]
        imp = jnp.where(forced, BIG, jnp.where(future, -BIG, imp))
        _, sel = lax.top_k(imp, topk)
        ks = gather(ks_blocks, sel)
        vs = gather(vs_blocks, sel)
        tok = sel[..., None] * SLC_BLOCK + jnp.arange(SLC_BLOCK)
        m_s = (tok <= tq[None, None, :, None, None]).reshape(B, G, 1, Cq, topk * SLC_BLOCK)
        s_s = jnp.einsum('bgpqd,bgqkld->bgpqkl', qc, ks).reshape(B, G, P, Cq, topk * SLC_BLOCK)
        p_s = masked_softmax(s_s, m_s).reshape(B, G, P, Cq, topk, SLC_BLOCK)
        o_s = jnp.einsum('bgpqkl,bgqkld->bgpqd', p_s.astype(vs.dtype), vs)

        kw = lax.dynamic_slice_in_dim(k_win_pad, t0, WINDOW + Cq, axis=2)
        vw = lax.dynamic_slice_in_dim(v_win_pad, t0, WINDOW + Cq, axis=2)
        pw = t0 - WINDOW + jnp.arange(WINDOW + Cq)
        m_w = (pw[None, :] <= tq[:, None]) & (pw[None, :] > tq[:, None] - WINDOW) & (pw[None, :] >= 0)
        s_w = jnp.einsum('bgpqd,bgkd->bgpqk', qc, kw)
        p_w = masked_softmax(s_w, m_w)
        o_w = jnp.einsum('bgpqk,bgkd->bgpqd', p_w.astype(vw.dtype), vw)

        return gc[..., 0:1] * o_c + gc[..., 1:2] * o_s + gc[..., 2:3] * o_w

    out = lax.map(chunk, jnp.arange(S // Cq))
    out = out.transpose(1, 0, 4, 2, 3, 5).reshape(B, S, NSA_HEADS * hd)
    return out @ w_out


def diff_mixer(h, w_in, lq1, lk1, lq2, lk2, subln_g, w_out, lambda_init):
    B, S, _ = h.shape
    H, hd = DIFF_HEADS, DIFF_HEAD_DIM
    q, k, v = jnp.split(h @ w_in, 3, axis=-1)
    q = q.reshape(B, S, H, 2, hd).transpose(0, 2, 3, 1, 4)
    k = k.reshape(B, S, H, 2, hd).transpose(0, 2, 3, 1, 4)
    v = v.reshape(B, S, H, 2 * hd).transpose(0, 2, 1, 3)
    cos, sin = rope_tables(jnp.arange(S), hd)
    q = apply_partial_rope(q, cos, sin) * jnp.asarray(hd ** -0.5, q.dtype)
    k = apply_partial_rope(k, cos, sin)
    f32 = jnp.float32
    lam = (jnp.exp(jnp.sum(lq1.astype(f32) * lk1.astype(f32))) -
           jnp.exp(jnp.sum(lq2.astype(f32) * lk2.astype(f32))) + lambda_init)
    kpos = jnp.arange(S)
    Qb = ATTN_Q_BLOCK

    def block(c):
        t0 = c * Qb
        tq = t0 + jnp.arange(Qb)
        qc = lax.dynamic_slice_in_dim(q, t0, Qb, axis=3)
        s = jnp.einsum('bhiqd,bhikd->bhiqk', qc, k)
        p = masked_softmax(s, kpos[None, :] <= tq[:, None])
        a = p[:, :, 0] - lam * p[:, :, 1]
        o = jnp.einsum('bhqk,bhkd->bhqd', a.astype(v.dtype), v)
        return rms_norm(o, subln_g) * jnp.asarray(1.0 - lambda_init, o.dtype)

    out = lax.map(block, jnp.arange(S // Qb))
    out = out.transpose(1, 0, 3, 2, 4).reshape(B, S, H * 2 * hd)
    return out @ w_out


def sqrelu_mlp(h, w_up, w_down):
    u = jax.nn.relu(h @ w_up)
    return (u * u) @ w_down


def setup_inputs(seed: int = 0) -> dict:
    key = jax.random.key(seed)
    ks = list(jax.random.split(key, 24))
    nrm = lambda k, shape, s: jax.random.normal(k, shape, jnp.float32) * s
    LA, LB, hd = N_NSA_LAYERS, N_DIFF_LAYERS, NSA_HEAD_DIM
    return {
        "x": nrm(ks[0], (BATCH, SEQ, D_MODEL), 1.0),
        "attn_norm_g": 1.0 + nrm(ks[1], (DEPTH, D_MODEL), 0.02),
        "mlp_norm_g": 1.0 + nrm(ks[2], (DEPTH, D_MODEL), 0.02),
        "nsa_w_in": nrm(ks[3], (LA, D_MODEL, NSA_IN_WIDTH), D_MODEL ** -0.5),
        "nsa_ck_pos": nrm(ks[4], (LA, CMP_BLOCK, hd), 0.1),
        "nsa_ck_w1": nrm(ks[5], (LA, CMP_BLOCK * hd, CMP_HIDDEN), (CMP_BLOCK * hd) ** -0.5),
        "nsa_ck_w2": nrm(ks[6], (LA, CMP_HIDDEN, hd), CMP_HIDDEN ** -0.5),
        "nsa_cv_pos": nrm(ks[7], (LA, CMP_BLOCK, hd), 0.1),
        "nsa_cv_w1": nrm(ks[8], (LA, CMP_BLOCK * hd, CMP_HIDDEN), (CMP_BLOCK * hd) ** -0.5),
        "nsa_cv_w2": nrm(ks[9], (LA, CMP_HIDDEN, hd), CMP_HIDDEN ** -0.5),
        "nsa_w_out": nrm(ks[10], (LA, D_MODEL, D_MODEL), D_MODEL ** -0.5),
        "diff_w_in": nrm(ks[11], (LB, D_MODEL, DIFF_IN_WIDTH), D_MODEL ** -0.5),
        "diff_lq1": nrm(ks[12], (LB, DIFF_HEAD_DIM), 0.1),
        "diff_lk1": nrm(ks[13], (LB, DIFF_HEAD_DIM), 0.1),
        "diff_lq2": nrm(ks[14], (LB, DIFF_HEAD_DIM), 0.1),
        "diff_lk2": nrm(ks[15], (LB, DIFF_HEAD_DIM), 0.1),
        "diff_subln_g": 1.0 + nrm(ks[16], (LB, 2 * DIFF_HEAD_DIM), 0.02),
        "diff_w_out": nrm(ks[17], (LB, D_MODEL, D_MODEL), D_MODEL ** -0.5),
        "mlp_w_up": nrm(ks[18], (DEPTH, D_MODEL, D_FF), D_MODEL ** -0.5),
        "mlp_w_down": nrm(ks[19], (DEPTH, D_FF, D_MODEL), D_FF ** -0.5),
        "final_norm_g": 1.0 + nrm(ks[20], (D_MODEL,), 0.02),
    }


def reference(x, attn_norm_g, mlp_norm_g, nsa_w_in, nsa_ck_pos, nsa_ck_w1, nsa_ck_w2,
              nsa_cv_pos, nsa_cv_w1, nsa_cv_w2, nsa_w_out, diff_w_in, diff_lq1, diff_lk1,
              diff_lq2, diff_lk2, diff_subln_g, diff_w_out, mlp_w_up, mlp_w_down, final_norm_g):
    for i in range(DEPTH):
        h = rms_norm(x, attn_norm_g[i])
        j = i // N_MIXERS
        if i % N_MIXERS == 0:
            mix = nsa_mixer(h, nsa_w_in[j], nsa_ck_pos[j], nsa_ck_w1[j], nsa_ck_w2[j],
                            nsa_cv_pos[j], nsa_cv_w1[j], nsa_cv_w2[j], nsa_w_out[j])
        else:
            lambda_init = 0.8 - 0.6 * math.exp(-0.3 * i)
            mix = diff_mixer(h, diff_w_in[j], diff_lq1[j], diff_lk1[j], diff_lq2[j], diff_lk2[j],
                             diff_subln_g[j], diff_w_out[j], lambda_init)
        x = x + mix
        x = x + sqrelu_mlp(rms_norm(x, mlp_norm_g[i]), mlp_w_up[i], mlp_w_down[i])
    return rms_norm(x, final_norm_g)
```

```python
import functools
import math

import jax
import jax.numpy as jnp
from jax import lax
from jax.experimental import pallas as pl
from jax.experimental.pallas import tpu as pltpu

F32 = jnp.float32
BF16 = jnp.bfloat16

EPS = 1e-6
ROPE_THETA = 500000.0
HEAD_DIM = 64
ROT_HALF = HEAD_DIM // 8
LANES = 128
NEG_INF = -1e30
BIG = 1e9

NSA_GROUPS = 4
NSA_HPG = 4
CMP_BLOCK = 32
CMP_STRIDE = 16
SLC_BLOCK = 64
SLC_TOPK = 16
WINDOW = 512

VMEM_LIMIT = 56 * 1024 * 1024

_NT = (((1,), (1,)), ((), ()))


def _params(sem):
    return pltpu.CompilerParams(dimension_semantics=sem, vmem_limit_bytes=VMEM_LIMIT)


def _rms(x, g):
    return x * lax.rsqrt(jnp.mean(x * x, axis=-1, keepdims=True) + EPS) * g


def _rope_tables(positions):
    n = positions.shape[0]
    rot = 2 * ROT_HALF
    inv = 1.0 / (ROPE_THETA ** (jnp.arange(0, rot, 2, dtype=F32) / rot))
    ang = positions.astype(F32)[:, None] * inv[None, :]
    cos, sin = jnp.cos(ang), jnp.sin(ang)
    rest = HEAD_DIM - rot
    c = jnp.concatenate([cos, cos, jnp.ones((n, rest), F32)], axis=1)
    sp = jnp.concatenate([jnp.zeros((n, ROT_HALF), F32), sin, jnp.zeros((n, rest), F32)], axis=1)
    sm = jnp.concatenate([-sin, jnp.zeros((n, HEAD_DIM - ROT_HALF), F32)], axis=1)
    reps = LANES // HEAD_DIM
    return tuple(jnp.tile(t, (1, reps)) for t in (c, sp, sm))


def _rope(y, c, sp, sm):
    return (y * c + pltpu.roll(y, ROT_HALF, 1) * sp
            + pltpu.roll(y, LANES - ROT_HALF, 1) * sm)


def _rope_wide(y, c, sp, sm):
    n = y.shape[1] // LANES
    return jnp.concatenate(
        [_rope(y[:, k * LANES:(k + 1) * LANES], c, sp, sm) for k in range(n)], axis=1)


def _nsa_proj_kernel(x_ref, g_ref, w_ref, c_ref, sp_ref, sm_ref,
                     q_ref, kvc_ref, ks_ref, vs_ref, kw_ref, vw_ref, gate_ref, *, tm, d):
    i = pl.program_id(1)
    h = _rms(x_ref[0], g_ref[...]).astype(BF16)
    c, sp, sm = c_ref[...], sp_ref[...], sm_ref[...]
    kvw = NSA_GROUPS * HEAD_DIM

    def proj(lo, width):
        return jnp.dot(h, w_ref[:, lo:lo + width], preferred_element_type=F32)

    q = proj(0, d)
    q_ref[0] = (_rope_wide(q, c, sp, sm) * (HEAD_DIM ** -0.5)).astype(BF16)
    kvc_ref[0] = proj(d, 2 * kvw).astype(BF16)

    ksl = _rope_wide(proj(d + 2 * kvw, kvw), c, sp, sm)
    vsl = proj(d + 3 * kvw, kvw)
    kwn = _rope_wide(proj(d + 4 * kvw, kvw), c, sp, sm)
    vwn = proj(d + 5 * kvw, kvw)
    nblk = ks_ref.shape[3] - HEAD_DIM
    blk = (lax.broadcasted_iota(jnp.int32, (tm, nblk), 0) // SLC_BLOCK) + i * (tm // SLC_BLOCK)
    onehot = jnp.where(blk == lax.broadcasted_iota(jnp.int32, (tm, nblk), 1), 1.0, 0.0)
    for g in range(NSA_GROUPS):
        sl = slice(g * HEAD_DIM, (g + 1) * HEAD_DIM)
        ks_ref[0, g] = jnp.concatenate([ksl[:, sl], onehot], axis=1).astype(BF16)
        vs_ref[0, g] = vsl[:, sl].astype(BF16)
        kw_ref[0, g] = kwn[:, sl].astype(BF16)
        vw_ref[0, g] = vwn[:, sl].astype(BF16)
    gate_ref[0] = jax.nn.sigmoid(proj(d + 6 * kvw, NSA_GROUPS * LANES))


def _nsa_proj(x, g, w_in, tables, tm):
    B, S, D = x.shape
    G, hd = NSA_GROUPS, HEAD_DIM
    kvw = G * hd
    nblk = S // SLC_BLOCK
    n_main = D + 6 * kvw
    wg = w_in[:, n_main:].reshape(D, G, NSA_HPG * 3)
    wg = jnp.pad(wg, ((0, 0), (0, 0), (0, LANES - NSA_HPG * 3))).reshape(D, G * LANES)
    w = jnp.concatenate([w_in[:, :n_main], wg], axis=1).astype(BF16)
    nw = w.shape[1]
    c, sp, sm = tables
    tab_spec = pl.BlockSpec((tm, LANES), lambda b, i: (i, 0))
    kv_spec = lambda width: pl.BlockSpec((1, G, tm, width), lambda b, i: (b, 0, i, 0))
    return pl.pallas_call(
        functools.partial(_nsa_proj_kernel, tm=tm, d=D),
        grid=(B, S // tm),
        in_specs=[
            pl.BlockSpec((1, tm, D), lambda b, i: (b, i, 0)),
            pl.BlockSpec((1, D), lambda b, i: (0, 0)),
            pl.BlockSpec((D, nw), lambda b, i: (0, 0)),
            tab_spec, tab_spec, tab_spec,
        ],
        out_specs=[
            pl.BlockSpec((1, tm, D), lambda b, i: (b, i, 0)),
            pl.BlockSpec((1, tm, 2 * kvw), lambda b, i: (b, i, 0)),
            kv_spec(hd + nblk), kv_spec(hd), kv_spec(hd), kv_spec(hd),
            pl.BlockSpec((1, tm, G * LANES), lambda b, i: (b, i, 0)),
        ],
        out_shape=[
            jax.ShapeDtypeStruct((B, S, D), BF16),
            jax.ShapeDtypeStruct((B, S, 2 * kvw), BF16),
            jax.ShapeDtypeStruct((B, G, S, hd + nblk), BF16),
            jax.ShapeDtypeStruct((B, G, S, hd), BF16),
            jax.ShapeDtypeStruct((B, G, S, hd), BF16),
            jax.ShapeDtypeStruct((B, G, S, hd), BF16),
            jax.ShapeDtypeStruct((B, S, G * LANES), F32),
        ],
        compiler_params=_params(("parallel", "parallel")),
        name="nsa_proj",
    )(x, g.reshape(1, D), w, c, sp, sm)


def _compress_kernel(r_ref, pa_ref, pb_ref, w1a_ref, w1b_ref, w2_ref, c_ref, sp_ref, sm_ref, o_ref):
    r = r_ref[0, 0, 0]
    rows = r.shape[0]
    a = jnp.dot(r, w1a_ref[0], preferred_element_type=F32)
    b = jnp.dot(r, w1b_ref[0], preferred_element_type=F32)
    bias = (jnp.dot(pa_ref[0], w1a_ref[0], preferred_element_type=F32)
            + jnp.dot(pb_ref[0], w1b_ref[0], preferred_element_type=F32))
    hid = a + pltpu.roll(b, rows - 1, 0) + bias[0:1]
    hid = jax.nn.gelu(hid).astype(BF16)
    y = jnp.dot(hid, w2_ref[0], preferred_element_type=F32)
    y = _rope(y, c_ref[0], sp_ref[0], sm_ref[0])
    o_ref[0, 0, 0] = y[:, :HEAD_DIM].astype(BF16)


def _compress(kvc, pos, w1, w2):
    B, S, _ = kvc.shape
    G, hd = NSA_GROUPS, HEAD_DIM
    rows = S // CMP_STRIDE
    rw = CMP_STRIDE * hd
    r = kvc.reshape(B, rows, CMP_STRIDE, 2, G, hd).transpose(3, 0, 4, 1, 2, 5).reshape(2, B, G, rows, rw)
    w1 = w1.astype(BF16)
    hidden = w1.shape[2]
    w1a, w1b = w1[:, :rw], w1[:, rw:]
    posr = pos.reshape(2, 2, 1, rw).astype(BF16)
    pa = jnp.broadcast_to(posr[:, 0], (2, 16, rw))
    pb = jnp.broadcast_to(posr[:, 1], (2, 16, rw))
    w2p = jnp.pad(w2, ((0, 0), (0, 0), (0, LANES - hd))).astype(BF16)
    cmp_end = jnp.arange(rows) * CMP_STRIDE + CMP_BLOCK - 1
    c, sp, sm = _rope_tables(cmp_end)
    ident = (jnp.ones_like(c), jnp.zeros_like(sp), jnp.zeros_like(sm))
    c, sp, sm = (jnp.stack([t, u]) for t, u in zip((c, sp, sm), ident))
    per_kind = lambda shape: pl.BlockSpec((1,) + shape, lambda k, b, g: (k,) + (0,) * len(shape))
    return pl.pallas_call(
        _compress_kernel,
        grid=(2, B, G),
        in_specs=[
            pl.BlockSpec((1, 1, 1, rows, rw), lambda k, b, g: (k, b, g, 0, 0)),
            per_kind((16, rw)), per_kind((16, rw)),
            per_kind((rw, hidden)), per_kind((rw, hidden)), per_kind((hidden, LANES)),
            per_kind((rows, LANES)), per_kind((rows, LANES)), per_kind((rows, LANES)),
        ],
        out_specs=pl.BlockSpec((1, 1, 1, rows, hd), lambda k, b, g: (k, b, g, 0, 0)),
        out_shape=jax.ShapeDtypeStruct((2, B, G, rows, hd), BF16),
        compiler_params=_params(("parallel", "parallel", "parallel")),
        name="nsa_compress",
    )(r, pa, pb, w1a, w1b, w2p, c, sp, sm)


def _softmax_rows(s, mask):
    s = jnp.where(mask, s, NEG_INF)
    m = jnp.max(s, axis=1, keepdims=True)
    p = jnp.where(mask, jnp.exp(s - m), 0.0)
    l = jnp.sum(p, axis=1, keepdims=True)
    return p, l


def _flash_causal(qa, k_at, v_at, t0, tq, tk, vdim):
    R = qa.shape[0]
    nfull = t0 // tk

    def step(j, carry, masked):
        m, l, acc = carry
        k = k_at(j)
        s = lax.dot_general(qa, k, _NT, preferred_element_type=F32)
        if masked:
            key = j * tk + lax.broadcasted_iota(jnp.int32, (R, tk), 1)
            tok = t0 + (lax.broadcasted_iota(jnp.int32, (R, tk), 0) & (tq - 1))
            s = jnp.where(key <= tok, s, NEG_INF)
        m_new = jnp.maximum(m, jnp.max(s, axis=1, keepdims=True))
        alpha = jnp.exp(m - m_new)
        p = jnp.exp(s - m_new)
        l = alpha * l + jnp.sum(p, axis=1, keepdims=True)
        acc = alpha * acc + jnp.dot(p.astype(BF16), v_at(j), preferred_element_type=F32)
        return m_new, l, acc

    init = (jnp.full((R, 1), NEG_INF, F32), jnp.zeros((R, 1), F32), jnp.zeros((R, vdim), F32))
    carry = lax.fori_loop(0, nfull, functools.partial(step, masked=False), init)
    _, l, acc = step(nfull, carry, True)
    return acc / l


def _nsa_attn_kernel(q_ref, gate_ref, kc_ref, vc_ref, selt_ref, ks_ref, vs_ref, kw_ref, vw_ref,
                     o_ref, *, tq, tk, topk):
    P, hd = NSA_HPG, HEAD_DIM
    R = P * tq
    t0 = pl.program_id(2) * tq
    q = q_ref[0]
    qs = jnp.concatenate([q[:, p * hd:(p + 1) * hd] for p in range(P)], axis=0)

    kc, vc = kc_ref[0, 0, 0], vc_ref[0, 0, 0]
    ncp = kc.shape[0]
    s = lax.dot_general(qs, kc, _NT, preferred_element_type=F32)
    tok = t0 + (lax.broadcasted_iota(jnp.int32, (R, ncp), 0) & (tq - 1))
    cmp_end = lax.broadcasted_iota(jnp.int32, (R, ncp), 1) * CMP_STRIDE + (CMP_BLOCK - 1)
    p_c, l_c = _softmax_rows(s, cmp_end <= tok)
    p_c = p_c / jnp.where(l_c > 0.0, l_c, 1.0)
    o_c = jnp.dot(p_c.astype(BF16), vc, preferred_element_type=F32)

    psum = p_c[0:tq]
    for p in range(1, P):
        psum = psum + p_c[p * tq:(p + 1) * tq]
    selt = selt_ref[...]
    ns = selt.shape[0]
    imp = jnp.zeros((ns, tq), F32)
    rem = psum
    for _ in range(3):
        part = rem.astype(BF16)
        imp = imp + lax.dot_general(selt, part, _NT, preferred_element_type=F32)
        rem = rem - part.astype(F32)
    jblk = lax.broadcasted_iota(jnp.int32, (ns, tq), 0)
    tcol = t0 + lax.broadcasted_iota(jnp.int32, (ns, tq), 1)
    cur = tcol // SLC_BLOCK
    forced = (jblk == 0) | (jblk == cur) | (jblk == cur - 1)
    future = jblk * SLC_BLOCK > tcol
    val = jnp.where(forced, BIG, jnp.where(future, -BIG, imp))
    sub = 8
    vals = [val[a:a + sub] for a in range(0, ns, sub)]
    ranks = [jnp.zeros((sub, tq), F32) for _ in vals]
    row = lax.broadcasted_iota(jnp.int32, (sub, tq), 0)
    for i in range(ns):
        vi = jnp.broadcast_to(val[i:i + 1, :], (sub, tq))
        for a, va in enumerate(vals):
            if a * sub > i:
                hit = jnp.where(vi >= va, 1.0, 0.0)
            elif a * sub + sub - 1 < i:
                hit = jnp.where(vi > va, 1.0, 0.0)
            else:
                hit = jnp.where(row > i - a * sub, jnp.where(vi >= va, 1.0, 0.0),
                                jnp.where(vi > va, 1.0, 0.0))
            ranks[a] = ranks[a] + hit
    pen_t = jnp.concatenate([jnp.where(r < topk, 0.0, NEG_INF) for r in ranks], axis=0)
    if ns < LANES:
        pen_t = jnp.concatenate([pen_t, jnp.zeros((LANES - ns, tq), F32)], axis=0)
    pen = pen_t.T[:, :ns].astype(BF16)
    qa = jnp.concatenate([qs, jnp.concatenate([pen] * P, axis=0)], axis=1)

    o_s = _flash_causal(
        qa,
        lambda j: ks_ref[0, 0, pl.ds(pl.multiple_of(j * tk, tk), tk), :],
        lambda j: vs_ref[0, 0, pl.ds(pl.multiple_of(j * tk, tk), tk), :],
        t0, tq, tk, hd)

    wk = WINDOW + tq
    wstart = pl.multiple_of(jnp.maximum(t0 - WINDOW, 0), tq)
    kw = kw_ref[0, 0, pl.ds(wstart, wk), :]
    vw = vw_ref[0, 0, pl.ds(wstart, wk), :]
    s = lax.dot_general(qs, kw, _NT, preferred_element_type=F32)
    tok = t0 + (lax.broadcasted_iota(jnp.int32, (R, wk), 0) & (tq - 1))
    key = wstart + lax.broadcasted_iota(jnp.int32, (R, wk), 1)
    p_w, l_w = _softmax_rows(s, (key <= tok) & (key > tok - WINDOW))
    o_w = jnp.dot(p_w.astype(BF16), vw, preferred_element_type=F32) / l_w

    gt = gate_ref[0]
    outs = []
    for p in range(P):
        sl = slice(p * tq, (p + 1) * tq)
        outs.append(gt[:, 3 * p:3 * p + 1] * o_c[sl] + gt[:, 3 * p + 1:3 * p + 2] * o_s[sl]
                    + gt[:, 3 * p + 2:3 * p + 3] * o_w[sl])
    o_ref[0] = jnp.concatenate(outs, axis=1).astype(BF16)


def _nsa_attn(q, gates, kvc_c, ks, vs, kw, vw, tq, tk):
    B, S, D = q.shape
    G, P, hd = NSA_GROUPS, NSA_HPG, HEAD_DIM
    ncp = kvc_c.shape[3]
    ns = S // SLC_BLOCK
    topk = min(SLC_TOPK, ns)
    n = jnp.arange(ncp)
    c_start, c_end = n * CMP_STRIDE, n * CMP_STRIDE + CMP_BLOCK - 1
    sj = jnp.arange(ns) * SLC_BLOCK
    selt = ((c_end[None, :] >= sj[:, None]) & (c_start[None, :] <= sj[:, None] + SLC_BLOCK - 1)
            & (n[None, :] < ncp - 1)).astype(BF16)
    full = lambda width: pl.BlockSpec((1, 1, S, width), lambda b, g, i: (b, g, 0, 0))
    return pl.pallas_call(
        functools.partial(_nsa_attn_kernel, tq=tq, tk=tk, topk=topk),
        grid=(B, G, S // tq),
        in_specs=[
            pl.BlockSpec((1, tq, P * hd), lambda b, g, i: (b, i, g)),
            pl.BlockSpec((1, tq, LANES), lambda b, g, i: (b, i, g)),
            pl.BlockSpec((1, 1, 1, ncp, hd), lambda b, g, i: (0, b, g, 0, 0)),
            pl.BlockSpec((1, 1, 1, ncp, hd), lambda b, g, i: (1, b, g, 0, 0)),
            pl.BlockSpec((ns, ncp), lambda b, g, i: (0, 0)),
            full(hd + ns), full(hd), full(hd), full(hd),
        ],
        out_specs=pl.BlockSpec((1, tq, P * hd), lambda b, g, i: (b, i, g)),
        out_shape=jax.ShapeDtypeStruct((B, S, D), BF16),
        compiler_params=_params(("parallel", "parallel", "arbitrary")),
        name="nsa_attn",
    )(q, gates, kvc_c, kvc_c, selt, ks, vs, kw, vw)


def _diff_proj_kernel(x_ref, g_ref, w_ref, c_ref, sp_ref, sm_ref, q_ref, k_ref, v_ref, *, d):
    h = _rms(x_ref[0], g_ref[...]).astype(BF16)
    c, sp, sm = c_ref[...], sp_ref[...], sm_ref[...]
    q = jnp.dot(h, w_ref[:, 0:d], preferred_element_type=F32)
    q_ref[0] = (_rope_wide(q, c, sp, sm) * (HEAD_DIM ** -0.5)).astype(BF16)
    k = jnp.dot(h, w_ref[:, d:2 * d], preferred_element_type=F32)
    k_ref[0] = _rope_wide(k, c, sp, sm).astype(BF16)
    v_ref[0] = jnp.dot(h, w_ref[:, 2 * d:3 * d], preferred_element_type=F32).astype(BF16)


def _diff_proj(x, g, w_in, tables, tm):
    B, S, D = x.shape
    c, sp, sm = tables
    tab_spec = pl.BlockSpec((tm, LANES), lambda b, i: (i, 0))
    tok_spec = pl.BlockSpec((1, tm, D), lambda b, i: (b, i, 0))
    return pl.pallas_call(
        functools.partial(_diff_proj_kernel, d=D),
        grid=(B, S // tm),
        in_specs=[tok_spec, pl.BlockSpec((1, D), lambda b, i: (0, 0)),
                  pl.BlockSpec((D, 3 * D), lambda b, i: (0, 0)), tab_spec, tab_spec, tab_spec],
        out_specs=[tok_spec, tok_spec, tok_spec],
        out_shape=[jax.ShapeDtypeStruct((B, S, D), BF16)] * 3,
        compiler_params=_params(("parallel", "parallel")),
        name="diff_proj",
    )(x, g.reshape(1, D), w_in.astype(BF16), c, sp, sm)


def _diff_attn_kernel(lq1_ref, lk1_ref, lq2_ref, lk2_ref, sg_ref, q_ref, k_ref, v_ref, o_ref,
                      *, tq, tk, lambda_init):
    t0 = pl.program_id(2) * tq
    lam = (jnp.exp(jnp.sum(lq1_ref[...] * lk1_ref[...], axis=1, keepdims=True))
           - jnp.exp(jnp.sum(lq2_ref[...] * lk2_ref[...], axis=1, keepdims=True)) + lambda_init)
    q = q_ref[0]
    lane = lax.broadcasted_iota(jnp.int32, q.shape, 1)
    zero = jnp.zeros_like(q)
    qa = jnp.concatenate([jnp.where(lane < HEAD_DIM, q, zero),
                          jnp.where(lane >= HEAD_DIM, q, zero)], axis=0)
    o = _flash_causal(
        qa,
        lambda j: k_ref[0, pl.ds(pl.multiple_of(j * tk, tk), tk), :],
        lambda j: v_ref[0, pl.ds(pl.multiple_of(j * tk, tk), tk), :],
        t0, tq, tk, 2 * HEAD_DIM)
    o = o[:tq] - lam * o[tq:]
    o_ref[0] = (_rms(o, sg_ref[...]) * (1.0 - lambda_init)).astype(BF16)


def _diff_attn(q, k, v, lq1, lk1, lq2, lk2, subln_g, lambda_init, tq, tk):
    B, S, D = q.shape
    hw = 2 * HEAD_DIM
    H = D // hw
    vec = lambda a: a.reshape(1, -1).astype(F32)
    vec_spec = lambda n: pl.BlockSpec((1, n), lambda b, h, i: (0, 0))
    kv_spec = pl.BlockSpec((1, S, hw), lambda b, h, i: (b, 0, h))
    q_spec = pl.BlockSpec((1, tq, hw), lambda b, h, i: (b, i, h))
    return pl.pallas_call(
        functools.partial(_diff_attn_kernel, tq=tq, tk=tk, lambda_init=lambda_init),
        grid=(B, H, S // tq),
        in_specs=[vec_spec(HEAD_DIM)] * 4 + [vec_spec(hw), q_spec, kv_spec, kv_spec],
        out_specs=q_spec,
        out_shape=jax.ShapeDtypeStruct((B, S, D), BF16),
        compiler_params=_params(("parallel", "parallel", "arbitrary")),
        name="diff_attn",
    )(vec(lq1), vec(lk1), vec(lq2), vec(lk2), vec(subln_g), q, k, v)


def _post_kernel(x_ref, a_ref, wo_ref, g_ref, wu_ref, wd_ref, fg_ref, o_ref, *, ff_chunk, final):
    x = x_ref[...] + jnp.dot(a_ref[...], wo_ref[...], preferred_element_type=F32)
    h = _rms(x, g_ref[...]).astype(BF16)
    y = x
    for lo in range(0, wu_ref.shape[1], ff_chunk):
        u = jnp.maximum(jnp.dot(h, wu_ref[:, lo:lo + ff_chunk], preferred_element_type=F32), 0.0)
        y = y + jnp.dot((u * u).astype(BF16), wd_ref[lo:lo + ff_chunk, :], preferred_element_type=F32)
    if final:
        y = _rms(y, fg_ref[...])
    o_ref[...] = y


def _post(x, a, w_out, g, w_up, w_down, final_g, tm, ff_chunk):
    B, S, D = x.shape
    T = B * S
    dff = w_up.shape[1]
    final = final_g is not None
    fg = (final_g if final else jnp.ones((D,), F32)).reshape(1, D)
    const = lambda shape: pl.BlockSpec(shape, lambda i: (0, 0), pipeline_mode=pl.Buffered(1))
    tok = pl.BlockSpec((tm, D), lambda i: (i, 0))
    out = pl.pallas_call(
        functools.partial(_post_kernel, ff_chunk=ff_chunk, final=final),
        grid=(T // tm,),
        in_specs=[tok, tok, const((D, D)), const((1, D)), const((D, dff)), const((dff, D)),
                  const((1, D))],
        out_specs=tok,
        out_shape=jax.ShapeDtypeStruct((T, D), F32),
        compiler_params=_params(("parallel",)),
        name="post_mlp",
    )(x.reshape(T, D), a.reshape(T, D), w_out.astype(BF16), g.reshape(1, D),
      w_up.astype(BF16), w_down.astype(BF16), fg)
    return out.reshape(B, S, D)


def kernel(x, attn_norm_g, mlp_norm_g, nsa_w_in, nsa_ck_pos, nsa_ck_w1, nsa_ck_w2, nsa_cv_pos, nsa_cv_w1, nsa_cv_w2, nsa_w_out, diff_w_in, diff_lq1, diff_lk1, diff_lq2, diff_lk2, diff_subln_g, diff_w_out, mlp_w_up, mlp_w_down, final_norm_g):
    B, S, D = x.shape
    depth = attn_norm_g.shape[0]
    tables = _rope_tables(jnp.arange(S))
    tm = min(512, S)
    for i in range(depth):
        j = i // 2
        if i % 2 == 0:
            q, kvc, ks, vs, kw, vw, gates = _nsa_proj(x, attn_norm_g[i], nsa_w_in[j], tables, tm)
            kvc_c = _compress(kvc,
                              jnp.stack([nsa_ck_pos[j], nsa_cv_pos[j]]),
                              jnp.stack([nsa_ck_w1[j], nsa_cv_w1[j]]),
                              jnp.stack([nsa_ck_w2[j], nsa_cv_w2[j]]))
            a = _nsa_attn(q, gates, kvc_c, ks, vs, kw, vw, tq=128, tk=min(512, S))
            w_out = nsa_w_out[j]
        else:
            lambda_init = 0.8 - 0.6 * math.exp(-0.3 * i)
            q, k, v = _diff_proj(x, attn_norm_g[i], diff_w_in[j], tables, tm)
            a = _diff_attn(q, k, v, diff_lq1[j], diff_lk1[j], diff_lq2[j], diff_lk2[j],
                           diff_subln_g[j], lambda_init, tq=min(256, S), tk=min(512, S))
            w_out = diff_w_out[j]
        x = _post(x, a, w_out, mlp_norm_g[i], mlp_w_up[i], mlp_w_down[i],
                  final_norm_g if i == depth - 1 else None, tm, ff_chunk=1024)
    return x
```

```python
import functools
import math

import jax
import jax.numpy as jnp
from jax import lax
from jax.experimental import pallas as pl
from jax.experimental.pallas import tpu as pltpu

F32 = jnp.float32
BF16 = jnp.bfloat16

EPS = 1e-6
ROPE_THETA = 500000.0
HEAD_DIM = 64
ROT_HALF = HEAD_DIM // 8
LANES = 128
NEG_INF = -1e30
BIG = 1e9
LOG2E = math.log2(math.e)
ONES_ROWS = 16

NSA_GROUPS = 4
NSA_HPG = 4
CMP_BLOCK = 32
CMP_STRIDE = 16
SLC_BLOCK = 64
SLC_TOPK = 16
WINDOW = 512
WIN_TILE = 128

VMEM_LIMIT = 56 * 1024 * 1024


def _params(sem):
    return pltpu.CompilerParams(dimension_semantics=sem, vmem_limit_bytes=VMEM_LIMIT)


def _rms(x, g):
    return x * lax.rsqrt(jnp.mean(x * x, axis=-1, keepdims=True) + EPS) * g


def _rope_tables(positions):
    n = positions.shape[0]
    rot = 2 * ROT_HALF
    inv = 1.0 / (ROPE_THETA ** (jnp.arange(0, rot, 2, dtype=F32) / rot))
    ang = positions.astype(F32)[:, None] * inv[None, :]
    cos, sin = jnp.cos(ang), jnp.sin(ang)
    rest = HEAD_DIM - rot
    c = jnp.concatenate([cos, cos, jnp.ones((n, rest), F32)], axis=1)
    sp = jnp.concatenate([jnp.zeros((n, ROT_HALF), F32), sin, jnp.zeros((n, rest), F32)], axis=1)
    sm = jnp.concatenate([-sin, jnp.zeros((n, HEAD_DIM - ROT_HALF), F32)], axis=1)
    reps = LANES // HEAD_DIM
    return tuple(jnp.tile(t, (1, reps)) for t in (c, sp, sm))


def _rope(y, c, sp, sm):
    return (y * c + pltpu.roll(y, ROT_HALF, 1) * sp
            + pltpu.roll(y, LANES - ROT_HALF, 1) * sm)


def _rope_wide(y, c, sp, sm):
    n = y.shape[1] // LANES
    return jnp.concatenate(
        [_rope(y[:, k * LANES:(k + 1) * LANES], c, sp, sm) for k in range(n)], axis=1)


def _with_ones_rows(vt):
    lead = vt.shape[:-2]
    tk = vt.shape[-1]
    ones = jnp.ones(lead + (1, tk), vt.dtype)
    zeros = jnp.zeros(lead + (ONES_ROWS - 1, tk), vt.dtype)
    return jnp.concatenate([vt, ones, zeros], axis=-2)


def _nsa_proj_kernel(x_ref, g_ref, w_ref, c_ref, sp_ref, sm_ref,
                     q_ref, kvc_ref, ks_ref, vs_ref, kw_ref, vw_ref, gate_ref, *, tm, d):
    i = pl.program_id(1)
    h = _rms(x_ref[0], g_ref[...]).astype(BF16)
    c, sp, sm = c_ref[...], sp_ref[...], sm_ref[...]
    kvw = NSA_GROUPS * HEAD_DIM

    def proj(lo, width):
        return jnp.dot(h, w_ref[:, lo:lo + width], preferred_element_type=F32)

    q = proj(0, d)
    q_ref[0] = (_rope_wide(q, c, sp, sm) * (HEAD_DIM ** -0.5 * LOG2E)).astype(BF16)
    kvc_ref[0] = proj(d, 2 * kvw).astype(BF16)

    ksl = _rope_wide(proj(d + 2 * kvw, kvw), c, sp, sm)
    vsl = proj(d + 3 * kvw, kvw)
    kwn = _rope_wide(proj(d + 4 * kvw, kvw), c, sp, sm)
    vwn = proj(d + 5 * kvw, kvw)
    nblk = ks_ref.shape[3] - HEAD_DIM
    blk = (lax.broadcasted_iota(jnp.int32, (tm, nblk), 0) // SLC_BLOCK) + i * (tm // SLC_BLOCK)
    onehot = jnp.where(blk == lax.broadcasted_iota(jnp.int32, (tm, nblk), 1), 1.0, 0.0)
    for g in range(NSA_GROUPS):
        sl = slice(g * HEAD_DIM, (g + 1) * HEAD_DIM)
        ks_ref[0, g] = jnp.concatenate([ksl[:, sl], onehot], axis=1).astype(BF16)
        vs_ref[0, g] = vsl[:, sl].astype(BF16)
        kw_ref[0, g] = kwn[:, sl].astype(BF16)
        vw_ref[0, g] = vwn[:, sl].astype(BF16)
    gate_ref[0] = jax.nn.sigmoid(proj(d + 6 * kvw, NSA_GROUPS * LANES))


def _nsa_proj(x, g, w_in, tables, tm):
    B, S, D = x.shape
    G, hd = NSA_GROUPS, HEAD_DIM
    kvw = G * hd
    nblk = S // SLC_BLOCK
    n_main = D + 6 * kvw
    wg = w_in[:, n_main:].reshape(D, G, NSA_HPG * 3)
    wg = jnp.pad(wg, ((0, 0), (0, 0), (0, LANES - NSA_HPG * 3))).reshape(D, G * LANES)
    w = jnp.concatenate([w_in[:, :n_main], wg], axis=1).astype(BF16)
    nw = w.shape[1]
    c, sp, sm = tables
    tab_spec = pl.BlockSpec((tm, LANES), lambda b, i: (i, 0))
    kv_spec = lambda width: pl.BlockSpec((1, G, tm, width), lambda b, i: (b, 0, i, 0))
    return pl.pallas_call(
        functools.partial(_nsa_proj_kernel, tm=tm, d=D),
        grid=(B, S // tm),
        in_specs=[
            pl.BlockSpec((1, tm, D), lambda b, i: (b, i, 0)),
            pl.BlockSpec((1, D), lambda b, i: (0, 0)),
            pl.BlockSpec((D, nw), lambda b, i: (0, 0)),
            tab_spec, tab_spec, tab_spec,
        ],
        out_specs=[
            pl.BlockSpec((1, tm, D), lambda b, i: (b, i, 0)),
            pl.BlockSpec((1, tm, 2 * kvw), lambda b, i: (b, i, 0)),
            kv_spec(hd + nblk), kv_spec(hd), kv_spec(hd), kv_spec(hd),
            pl.BlockSpec((1, tm, G * LANES), lambda b, i: (b, i, 0)),
        ],
        out_shape=[
            jax.ShapeDtypeStruct((B, S, D), BF16),
            jax.ShapeDtypeStruct((B, S, 2 * kvw), BF16),
            jax.ShapeDtypeStruct((B, G, S, hd + nblk), BF16),
            jax.ShapeDtypeStruct((B, G, S, hd), BF16),
            jax.ShapeDtypeStruct((B, G, S, hd), BF16),
            jax.ShapeDtypeStruct((B, G, S, hd), BF16),
            jax.ShapeDtypeStruct((B, S, G * LANES), F32),
        ],
        compiler_params=_params(("parallel", "parallel")),
        name="nsa_proj",
    )(x, g.reshape(1, D), w, c, sp, sm)


def _compress_kernel(r_ref, pa_ref, pb_ref, w1a_ref, w1b_ref, w2_ref, c_ref, sp_ref, sm_ref, o_ref):
    r = r_ref[0, 0, 0]
    rows = r.shape[0]
    a = jnp.dot(r, w1a_ref[0], preferred_element_type=F32)
    b = jnp.dot(r, w1b_ref[0], preferred_element_type=F32)
    bias = (jnp.dot(pa_ref[0], w1a_ref[0], preferred_element_type=F32)
            + jnp.dot(pb_ref[0], w1b_ref[0], preferred_element_type=F32))
    hid = a + pltpu.roll(b, rows - 1, 0) + bias[0:1]
    hid = jax.nn.gelu(hid).astype(BF16)
    y = jnp.dot(hid, w2_ref[0], preferred_element_type=F32)
    y = _rope(y, c_ref[0], sp_ref[0], sm_ref[0])
    o_ref[0, 0, 0] = y[:, :HEAD_DIM].astype(BF16)


def _compress(kvc, pos, w1, w2):
    B, S, _ = kvc.shape
    G, hd = NSA_GROUPS, HEAD_DIM
    rows = S // CMP_STRIDE
    rw = CMP_STRIDE * hd
    r = kvc.reshape(B, rows, CMP_STRIDE, 2, G, hd).transpose(3, 0, 4, 1, 2, 5).reshape(2, B, G, rows, rw)
    w1 = w1.astype(BF16)
    hidden = w1.shape[2]
    w1a, w1b = w1[:, :rw], w1[:, rw:]
    posr = pos.reshape(2, 2, 1, rw).astype(BF16)
    pa = jnp.broadcast_to(posr[:, 0], (2, 16, rw))
    pb = jnp.broadcast_to(posr[:, 1], (2, 16, rw))
    w2p = jnp.pad(w2, ((0, 0), (0, 0), (0, LANES - hd))).astype(BF16)
    cmp_end = jnp.arange(rows) * CMP_STRIDE + CMP_BLOCK - 1
    c, sp, sm = _rope_tables(cmp_end)
    ident = (jnp.ones_like(c), jnp.zeros_like(sp), jnp.zeros_like(sm))
    c, sp, sm = (jnp.stack([t, u]) for t, u in zip((c, sp, sm), ident))
    per_kind = lambda shape: pl.BlockSpec((1,) + shape, lambda k, b, g: (k,) + (0,) * len(shape))
    return pl.pallas_call(
        _compress_kernel,
        grid=(2, B, G),
        in_specs=[
            pl.BlockSpec((1, 1, 1, rows, rw), lambda k, b, g: (k, b, g, 0, 0)),
            per_kind((16, rw)), per_kind((16, rw)),
            per_kind((rw, hidden)), per_kind((rw, hidden)), per_kind((hidden, LANES)),
            per_kind((rows, LANES)), per_kind((rows, LANES)), per_kind((rows, LANES)),
        ],
        out_specs=pl.BlockSpec((1, 1, 1, rows, hd), lambda k, b, g: (k, b, g, 0, 0)),
        out_shape=jax.ShapeDtypeStruct((2, B, G, rows, hd), BF16),
        compiler_params=_params(("parallel", "parallel", "parallel")),
        name="nsa_compress",
    )(r, pa, pb, w1a, w1b, w2p, c, sp, sm)


def _flash_scratch(chains, tk, r, vrows):
    return [pltpu.VMEM((chains, 2, tk, r), F32), pltpu.VMEM((chains, 2, tk, r), BF16),
            pltpu.VMEM((chains, 1, r), F32), pltpu.VMEM((chains, 2, 1, r), F32),
            pltpu.VMEM((chains, vrows, r), F32)]


def _flash_causal_t(qats, k_at, vt_at, t0, tq, tk, scratch):
    s_ref, p_ref, m_ref, al_ref, acc_ref = scratch
    chains = range(len(qats))
    R = qats[0].shape[1]
    nfull = t0 // tk
    n = nfull + 1
    tile = lambda u: jnp.where(u == 0, nfull, jnp.maximum(u - 1, 0))

    def scores(u, slot):
        for c in chains:
            s_ref[c, slot] = jnp.dot(k_at(c, tile(u)), qats[c], preferred_element_type=F32)

    def probs(slot, masked=False):
        for c in chains:
            s = s_ref[c, slot]
            if masked:
                key = nfull * tk + lax.broadcasted_iota(jnp.int32, (tk, R), 0)
                tok = t0 + (lax.broadcasted_iota(jnp.int32, (tk, R), 1) & (tq - 1))
                s = jnp.where(key <= tok, s, NEG_INF)
            m = m_ref[c]
            m_new = jnp.maximum(m, jnp.max(s, axis=0, keepdims=True))
            m_ref[c] = m_new
            al_ref[c, slot] = jnp.exp2(m - m_new)
            p_ref[c, slot] = jnp.exp2(s - m_new).astype(BF16)

    def accum(u, slot):
        for c in chains:
            acc_ref[c] = al_ref[c, slot] * acc_ref[c] + jnp.dot(
                vt_at(c, tile(u)), p_ref[c, slot], preferred_element_type=F32)

    def stage(u, a):
        scores(u + 2, a)
        probs(1 - a)
        accum(u, a)

    m_ref[...] = jnp.full(m_ref.shape, NEG_INF, F32)
    acc_ref[...] = jnp.zeros(acc_ref.shape, F32)
    scores(0, 0)
    probs(0, masked=True)
    scores(1, 1)

    pairs = jnp.maximum(n - 2, 0) // 2

    def body(k, carry):
        stage(2 * k, 0)
        stage(2 * k + 1, 1)
        return carry

    lax.fori_loop(0, pairs, body, 0)

    @pl.when(n == 1)
    def _():
        accum(0, 0)

    @pl.when((n >= 2) & (n % 2 == 0))
    def _():
        accum(n - 2, 0)
        probs(1)
        accum(n - 1, 1)

    @pl.when((n >= 3) & (n % 2 == 1))
    def _():
        stage(n - 3, 0)
        accum(n - 2, 1)
        probs(0)
        accum(n - 1, 0)


def _block_penalty(imp, t0, tq, topk):
    ns = imp.shape[0]
    jblk = lax.broadcasted_iota(jnp.int32, (ns, tq), 0)
    tcol = t0 + lax.broadcasted_iota(jnp.int32, (ns, tq), 1)
    cur = tcol // SLC_BLOCK
    forced = (jblk == 0) | (jblk == cur) | (jblk == cur - 1)
    future = jblk * SLC_BLOCK > tcol
    val = jnp.where(forced, BIG, jnp.where(future, -BIG, imp))
    sub = 8
    vals = [val[a:a + sub] for a in range(0, ns, sub)]
    ranks = [jnp.zeros((sub, tq), F32) for _ in vals]
    row = lax.broadcasted_iota(jnp.int32, (sub, tq), 0)
    for i in range(ns):
        vi = jnp.broadcast_to(val[i:i + 1, :], (sub, tq))
        for a, va in enumerate(vals):
            if a * sub > i:
                hit = jnp.where(vi >= va, 1.0, 0.0)
            elif a * sub + sub - 1 < i:
                hit = jnp.where(vi > va, 1.0, 0.0)
            else:
                hit = jnp.where(row > i - a * sub, jnp.where(vi >= va, 1.0, 0.0),
                                jnp.where(vi > va, 1.0, 0.0))
            ranks[a] = ranks[a] + hit
    return jnp.concatenate([jnp.where(r < topk, 0.0, NEG_INF) for r in ranks], axis=0)


def _nsa_attn_kernel(q_ref, gate_ref, kc_ref, vct_ref, selt_ref, ks_ref, vst_ref, kw_ref, vwt_ref,
                     o_ref, pre_ref, *scratch, tq, tk, groups, topk):
    P, hd = NSA_HPG, HEAD_DIM
    R = P * tq
    qw = P * hd
    t0 = pl.program_id(2) * tq
    selt = selt_ref[...]
    ncp = selt.shape[1]
    wk = WINDOW + tq
    wstart = pl.multiple_of(jnp.maximum(t0 - WINDOW, 0), tq)
    gate_t = lambda c: gate_ref[0, :, c * LANES:(c + 1) * LANES].T
    cols = lambda p: slice(p * tq, (p + 1) * tq)

    qats = []
    for c in range(groups):
        qt = q_ref[0, :, c * qw:(c + 1) * qw].astype(F32).T
        q_t = jnp.concatenate([qt[p * hd:(p + 1) * hd] for p in range(P)], axis=1).astype(BF16)
        gt = gate_t(c)

        s = jnp.dot(kc_ref[0, 0, c], q_t, preferred_element_type=F32)
        tok = t0 + (lax.broadcasted_iota(jnp.int32, (ncp, R), 1) & (tq - 1))
        cmp_end = lax.broadcasted_iota(jnp.int32, (ncp, R), 0) * CMP_STRIDE + (CMP_BLOCK - 1)
        mask = cmp_end <= tok
        s = jnp.where(mask, s, NEG_INF)
        p_c = jnp.where(mask, jnp.exp2(s - jnp.max(s, axis=0, keepdims=True)), 0.0)
        l_c = jnp.sum(p_c, axis=0, keepdims=True)
        p_c = p_c / jnp.where(l_c > 0.0, l_c, 1.0)
        o_c = jnp.dot(vct_ref[0, c], p_c.astype(BF16), preferred_element_type=F32)

        rem = p_c[:, cols(0)]
        for p in range(1, P):
            rem = rem + p_c[:, cols(p)]
        imp = jnp.zeros((selt.shape[0], tq), F32)
        for _ in range(3):
            part = rem.astype(BF16)
            imp = imp + jnp.dot(selt, part, preferred_element_type=F32)
            rem = rem - part.astype(F32)
        pen = _block_penalty(imp, t0, tq, topk).astype(BF16)
        qats.append(jnp.concatenate([q_t, jnp.concatenate([pen] * P, axis=1)], axis=0))

        s = jnp.dot(kw_ref[0, c, pl.ds(wstart, wk), :], q_t, preferred_element_type=F32)
        tok = t0 + (lax.broadcasted_iota(jnp.int32, (wk, R), 1) & (tq - 1))
        key = wstart + lax.broadcasted_iota(jnp.int32, (wk, R), 0)
        s = jnp.where((key <= tok) & (key > tok - WINDOW), s, NEG_INF)
        p_w = jnp.exp2(s - jnp.max(s, axis=0, keepdims=True)).astype(BF16)
        wt0 = wstart // WIN_TILE
        acc_w = jnp.zeros((hd + ONES_ROWS, R), F32)
        for x in range(wk // WIN_TILE):
            acc_w = acc_w + jnp.dot(vwt_ref[0, c, wt0 + x], p_w[x * WIN_TILE:(x + 1) * WIN_TILE],
                                    preferred_element_type=F32)
        o_w = acc_w[:hd] / acc_w[hd:hd + 1]

        for p in range(P):
            pre_ref[c, p * hd:(p + 1) * hd, :] = (gt[3 * p:3 * p + 1] * o_c[:, cols(p)]
                                                  + gt[3 * p + 2:3 * p + 3] * o_w[:, cols(p)])

    _flash_causal_t(
        qats,
        lambda c, j: ks_ref[0, c, pl.ds(pl.multiple_of(j * tk, tk), tk), :],
        lambda c, j: vst_ref[0, c, j],
        t0, tq, tk, scratch)

    acc_ref = scratch[-1]
    for c in range(groups):
        acc = acc_ref[c]
        o_s = acc[:hd] / acc[hd:hd + 1]
        gt = gate_t(c)
        out_t = jnp.concatenate(
            [pre_ref[c, p * hd:(p + 1) * hd, :] + gt[3 * p + 1:3 * p + 2] * o_s[:, cols(p)]
             for p in range(P)], axis=0)
        o_ref[0, :, c * qw:(c + 1) * qw] = out_t.T.astype(BF16)


def _nsa_attn(q, gates, kvc_c, ks, vs, kw, vw, tq, tk):
    B, S, D = q.shape
    G, P, hd = NSA_GROUPS, NSA_HPG, HEAD_DIM
    groups = 2
    ncp = kvc_c.shape[3]
    ns = S // SLC_BLOCK
    nt = S // tk
    nwt = S // WIN_TILE
    topk = min(SLC_TOPK, ns)
    n = jnp.arange(ncp)
    c_start, c_end = n * CMP_STRIDE, n * CMP_STRIDE + CMP_BLOCK - 1
    sj = jnp.arange(ns) * SLC_BLOCK
    selt = ((c_end[None, :] >= sj[:, None]) & (c_start[None, :] <= sj[:, None] + SLC_BLOCK - 1)
            & (n[None, :] < ncp - 1)).astype(BF16)
    vct = kvc_c[1].transpose(0, 1, 3, 2)
    vst = _with_ones_rows(vs.reshape(B, G, nt, tk, hd).transpose(0, 1, 2, 4, 3))
    vwt = _with_ones_rows(vw.reshape(B, G, nwt, WIN_TILE, hd).transpose(0, 1, 2, 4, 3))
    vrows = hd + ONES_ROWS
    tok_spec = lambda width: pl.BlockSpec((1, tq, groups * width), lambda b, g, i: (b, i, g))
    return pl.pallas_call(
        functools.partial(_nsa_attn_kernel, tq=tq, tk=tk, groups=groups, topk=topk),
        grid=(B, G // groups, S // tq),
        in_specs=[
            tok_spec(P * hd),
            tok_spec(LANES),
            pl.BlockSpec((1, 1, groups, ncp, hd), lambda b, g, i: (0, b, g, 0, 0)),
            pl.BlockSpec((1, groups, hd, ncp), lambda b, g, i: (b, g, 0, 0)),
            pl.BlockSpec((ns, ncp), lambda b, g, i: (0, 0)),
            pl.BlockSpec((1, groups, S, hd + ns), lambda b, g, i: (b, g, 0, 0)),
            pl.BlockSpec((1, groups, nt, vrows, tk), lambda b, g, i: (b, g, 0, 0, 0)),
            pl.BlockSpec((1, groups, S, hd), lambda b, g, i: (b, g, 0, 0)),
            pl.BlockSpec((1, groups, nwt, vrows, WIN_TILE), lambda b, g, i: (b, g, 0, 0, 0)),
        ],
        out_specs=tok_spec(P * hd),
        out_shape=jax.ShapeDtypeStruct((B, S, D), BF16),
        scratch_shapes=[pltpu.VMEM((groups, P * hd, tq), F32)]
        + _flash_scratch(groups, tk, P * tq, vrows),
        compiler_params=_params(("parallel", "parallel", "arbitrary")),
        name="nsa_attn",
    )(q, gates, kvc_c, vct, selt, ks, vst, kw, vwt)


def _diff_proj_kernel(x_ref, g_ref, w_ref, c_ref, sp_ref, sm_ref, q_ref, k_ref, v_ref, *, d):
    h = _rms(x_ref[0], g_ref[...]).astype(BF16)
    c, sp, sm = c_ref[...], sp_ref[...], sm_ref[...]
    q = jnp.dot(h, w_ref[:, 0:d], preferred_element_type=F32)
    q_ref[0] = (_rope_wide(q, c, sp, sm) * (HEAD_DIM ** -0.5 * LOG2E)).astype(BF16)
    k = jnp.dot(h, w_ref[:, d:2 * d], preferred_element_type=F32)
    k_ref[0] = _rope_wide(k, c, sp, sm).astype(BF16)
    v_ref[0] = jnp.dot(h, w_ref[:, 2 * d:3 * d], preferred_element_type=F32).astype(BF16)


def _diff_proj(x, g, w_in, tables, tm):
    B, S, D = x.shape
    c, sp, sm = tables
    tab_spec = pl.BlockSpec((tm, LANES), lambda b, i: (i, 0))
    tok_spec = pl.BlockSpec((1, tm, D), lambda b, i: (b, i, 0))
    return pl.pallas_call(
        functools.partial(_diff_proj_kernel, d=D),
        grid=(B, S // tm),
        in_specs=[tok_spec, pl.BlockSpec((1, D), lambda b, i: (0, 0)),
                  pl.BlockSpec((D, 3 * D), lambda b, i: (0, 0)), tab_spec, tab_spec, tab_spec],
        out_specs=[tok_spec, tok_spec, tok_spec],
        out_shape=[jax.ShapeDtypeStruct((B, S, D), BF16)] * 3,
        compiler_params=_params(("parallel", "parallel")),
        name="diff_proj",
    )(x, g.reshape(1, D), w_in.astype(BF16), c, sp, sm)


def _diff_attn_kernel(lq1_ref, lk1_ref, lq2_ref, lk2_ref, sg_ref, q_ref, k_ref, vt_ref, o_ref,
                      *scratch, tq, tk, heads, lambda_init):
    hw = 2 * HEAD_DIM
    t0 = pl.program_id(2) * tq
    lam = (jnp.exp(jnp.sum(lq1_ref[...] * lk1_ref[...], axis=1, keepdims=True))
           - jnp.exp(jnp.sum(lq2_ref[...] * lk2_ref[...], axis=1, keepdims=True)) + lambda_init)
    qats = []
    for c in range(heads):
        qt = q_ref[0, :, c * hw:(c + 1) * hw].astype(F32).T
        row = lax.broadcasted_iota(jnp.int32, qt.shape, 0)
        qats.append(jnp.concatenate([jnp.where(row < HEAD_DIM, qt, 0.0),
                                     jnp.where(row >= HEAD_DIM, qt, 0.0)], axis=1).astype(BF16))
    _flash_causal_t(
        qats,
        lambda c, j: k_ref[0, pl.ds(pl.multiple_of(j * tk, tk), tk), c * hw:(c + 1) * hw],
        lambda c, j: vt_ref[0, c, j],
        t0, tq, tk, scratch)
    acc_ref = scratch[-1]
    for c in range(heads):
        acc = acc_ref[c]
        o = acc[:hw] / acc[hw:hw + 1]
        o = o[:, :tq] - lam * o[:, tq:]
        o = o * lax.rsqrt(jnp.mean(o * o, axis=0, keepdims=True) + EPS) * sg_ref[...]
        o_ref[0, :, c * hw:(c + 1) * hw] = (o * (1.0 - lambda_init)).T.astype(BF16)


def _diff_attn(q, k, v, lq1, lk1, lq2, lk2, subln_g, lambda_init, tq, tk):
    B, S, D = q.shape
    hw = 2 * HEAD_DIM
    H = D // hw
    nt = S // tk
    vt = _with_ones_rows(v.reshape(B, nt, tk, H, hw).transpose(0, 3, 1, 4, 2))
    heads = 2
    vec = lambda a: a.reshape(1, -1).astype(F32)
    vec_spec = lambda n: pl.BlockSpec((1, n), lambda b, h, i: (0, 0))
    q_spec = pl.BlockSpec((1, tq, heads * hw), lambda b, h, i: (b, i, h))
    return pl.pallas_call(
        functools.partial(_diff_attn_kernel, tq=tq, tk=tk, heads=heads, lambda_init=lambda_init),
        grid=(B, H // heads, S // tq),
        in_specs=[vec_spec(HEAD_DIM)] * 4 + [
            pl.BlockSpec((hw, 1), lambda b, h, i: (0, 0)),
            q_spec,
            pl.BlockSpec((1, S, heads * hw), lambda b, h, i: (b, 0, h)),
            pl.BlockSpec((1, heads, nt, hw + ONES_ROWS, tk), lambda b, h, i: (b, h, 0, 0, 0)),
        ],
        out_specs=q_spec,
        out_shape=jax.ShapeDtypeStruct((B, S, D), BF16),
        scratch_shapes=_flash_scratch(heads, tk, 2 * tq, hw + ONES_ROWS),
        compiler_params=_params(("parallel", "parallel", "arbitrary")),
        name="diff_attn",
    )(vec(lq1), vec(lk1), vec(lq2), vec(lk2), subln_g.reshape(hw, 1).astype(F32), q, k, vt)


def _post_kernel(x_ref, a_ref, wo_ref, g_ref, wu_ref, wd_ref, fg_ref, o_ref, *, ff_chunk, final):
    x = x_ref[...] + jnp.dot(a_ref[...], wo_ref[...], preferred_element_type=F32)
    h = _rms(x, g_ref[...]).astype(BF16)
    y = x
    for lo in range(0, wu_ref.shape[1], ff_chunk):
        u = jnp.maximum(jnp.dot(h, wu_ref[:, lo:lo + ff_chunk], preferred_element_type=F32), 0.0)
        y = y + jnp.dot((u * u).astype(BF16), wd_ref[lo:lo + ff_chunk, :], preferred_element_type=F32)
    if final:
        y = _rms(y, fg_ref[...])
    o_ref[...] = y


def _post(x, a, w_out, g, w_up, w_down, final_g, tm, ff_chunk):
    B, S, D = x.shape
    T = B * S
    dff = w_up.shape[1]
    final = final_g is not None
    fg = (final_g if final else jnp.ones((D,), F32)).reshape(1, D)
    const = lambda shape: pl.BlockSpec(shape, lambda i: (0, 0), pipeline_mode=pl.Buffered(1))
    tok = pl.BlockSpec((tm, D), lambda i: (i, 0))
    out = pl.pallas_call(
        functools.partial(_post_kernel, ff_chunk=ff_chunk, final=final),
        grid=(T // tm,),
        in_specs=[tok, tok, const((D, D)), const((1, D)), const((D, dff)), const((dff, D)),
                  const((1, D))],
        out_specs=tok,
        out_shape=jax.ShapeDtypeStruct((T, D), F32),
        compiler_params=_params(("parallel",)),
        name="post_mlp",
    )(x.reshape(T, D), a.reshape(T, D), w_out.astype(BF16), g.reshape(1, D),
      w_up.astype(BF16), w_down.astype(BF16), fg)
    return out.reshape(B, S, D)


def kernel(x, attn_norm_g, mlp_norm_g, nsa_w_in, nsa_ck_pos, nsa_ck_w1, nsa_ck_w2, nsa_cv_pos, nsa_cv_w1, nsa_cv_w2, nsa_w_out, diff_w_in, diff_lq1, diff_lk1, diff_lq2, diff_lk2, diff_subln_g, diff_w_out, mlp_w_up, mlp_w_down, final_norm_g):
    B, S, D = x.shape
    depth = attn_norm_g.shape[0]
    tables = _rope_tables(jnp.arange(S))
    tm = min(512, S)
    for i in range(depth):
        j = i // 2
        if i % 2 == 0:
            q, kvc, ks, vs, kw, vw, gates = _nsa_proj(x, attn_norm_g[i], nsa_w_in[j], tables, tm)
            kvc_c = _compress(kvc,
                              jnp.stack([nsa_ck_pos[j], nsa_cv_pos[j]]),
                              jnp.stack([nsa_ck_w1[j], nsa_cv_w1[j]]),
                              jnp.stack([nsa_ck_w2[j], nsa_cv_w2[j]]))
            a = _nsa_attn(q, gates, kvc_c, ks, vs, kw, vw, tq=128, tk=min(512, S))
            w_out = nsa_w_out[j]
        else:
            lambda_init = 0.8 - 0.6 * math.exp(-0.3 * i)
            q, k, v = _diff_proj(x, attn_norm_g[i], diff_w_in[j], tables, tm)
            a = _diff_attn(q, k, v, diff_lq1[j], diff_lk1[j], diff_lq2[j], diff_lk2[j],
                           diff_subln_g[j], lambda_init, tq=min(256, S), tk=min(512, S))
            w_out = diff_w_out[j]
        x = _post(x, a, w_out, mlp_norm_g[i], mlp_w_up[i], mlp_w_down[i],
                  final_norm_g if i == depth - 1 else None, tm, ff_chunk=1024)
    return x
```

```python
import functools
import math

import jax
import jax.numpy as jnp
from jax import lax
from jax.experimental import pallas as pl
from jax.experimental.pallas import tpu as pltpu

F32 = jnp.float32
BF16 = jnp.bfloat16

EPS = 1e-6
ROPE_THETA = 500000.0
HEAD_DIM = 64
ROT_HALF = HEAD_DIM // 8
LANES = 128
NEG_INF = -1e30
BIG = 1e9
LOG2E = math.log2(math.e)
ONES_ROWS = 16

NSA_GROUPS = 4
NSA_HPG = 4
CMP_BLOCK = 32
CMP_STRIDE = 16
SLC_BLOCK = 64
SLC_TOPK = 16
WINDOW = 512
WIN_TILE = 128

VMEM_LIMIT = 56 * 1024 * 1024


def _params(sem):
    return pltpu.CompilerParams(dimension_semantics=sem, vmem_limit_bytes=VMEM_LIMIT)


def _rms(x, g):
    return x * lax.rsqrt(jnp.mean(x * x, axis=-1, keepdims=True) + EPS) * g


def _rope_tables(positions):
    n = positions.shape[0]
    rot = 2 * ROT_HALF
    inv = 1.0 / (ROPE_THETA ** (jnp.arange(0, rot, 2, dtype=F32) / rot))
    ang = positions.astype(F32)[:, None] * inv[None, :]
    cos, sin = jnp.cos(ang), jnp.sin(ang)
    rest = HEAD_DIM - rot
    c = jnp.concatenate([cos, cos, jnp.ones((n, rest), F32)], axis=1)
    sp = jnp.concatenate([jnp.zeros((n, ROT_HALF), F32), sin, jnp.zeros((n, rest), F32)], axis=1)
    sm = jnp.concatenate([-sin, jnp.zeros((n, HEAD_DIM - ROT_HALF), F32)], axis=1)
    reps = LANES // HEAD_DIM
    return tuple(jnp.tile(t, (1, reps)) for t in (c, sp, sm))


def _rope(y, c, sp, sm):
    return (y * c + pltpu.roll(y, ROT_HALF, 1) * sp
            + pltpu.roll(y, LANES - ROT_HALF, 1) * sm)


def _rope_wide(y, c, sp, sm):
    n = y.shape[1] // LANES
    return jnp.concatenate(
        [_rope(y[:, k * LANES:(k + 1) * LANES], c, sp, sm) for k in range(n)], axis=1)


def _ones_rows(width):
    row = lax.broadcasted_iota(jnp.int32, (ONES_ROWS, width), 0)
    return jnp.where(row == 0, 1.0, 0.0).astype(BF16)


def _nsa_proj_kernel(x_ref, g_ref, w_ref, c_ref, sp_ref, sm_ref,
                     q_ref, kvc_ref, ks_ref, vs_ref, kw_ref, vw_ref, gate_ref, *, tm, d):
    i = pl.program_id(1)
    h = _rms(x_ref[0], g_ref[...]).astype(BF16)
    c, sp, sm = c_ref[...], sp_ref[...], sm_ref[...]
    kvw = NSA_GROUPS * HEAD_DIM

    def proj(lo, width):
        return jnp.dot(h, w_ref[:, lo:lo + width], preferred_element_type=F32)

    q = proj(0, d)
    q_ref[0] = (_rope_wide(q, c, sp, sm) * (HEAD_DIM ** -0.5 * LOG2E)).astype(BF16)
    kvc_ref[0] = proj(d, 2 * kvw).astype(BF16)

    ksl = _rope_wide(proj(d + 2 * kvw, kvw), c, sp, sm)
    vsl = proj(d + 3 * kvw, kvw)
    kwn = _rope_wide(proj(d + 4 * kvw, kvw), c, sp, sm)
    vwn = proj(d + 5 * kvw, kvw)
    nblk = ks_ref.shape[3] - HEAD_DIM
    blk = (lax.broadcasted_iota(jnp.int32, (tm, nblk), 0) // SLC_BLOCK) + i * (tm // SLC_BLOCK)
    onehot = jnp.where(blk == lax.broadcasted_iota(jnp.int32, (tm, nblk), 1), 1.0, 0.0)
    vsl_t, vwn_t = vsl.T.astype(BF16), vwn.T.astype(BF16)
    ones_rows = _ones_rows(tm)
    for g in range(NSA_GROUPS):
        sl = slice(g * HEAD_DIM, (g + 1) * HEAD_DIM)
        ks_ref[0, g] = jnp.concatenate([ksl[:, sl], onehot], axis=1).astype(BF16)
        kw_ref[0, g] = kwn[:, sl].astype(BF16)
        vs_ref[0, g, 0, :HEAD_DIM, :] = vsl_t[sl]
        vs_ref[0, g, 0, HEAD_DIM:, :] = ones_rows
        for x in range(tm // WIN_TILE):
            vw_ref[0, g, x, :HEAD_DIM, :] = vwn_t[sl, x * WIN_TILE:(x + 1) * WIN_TILE]
            vw_ref[0, g, x, HEAD_DIM:, :] = ones_rows[:, :WIN_TILE]
    gate_ref[0] = jax.nn.sigmoid(proj(d + 6 * kvw, NSA_GROUPS * LANES))


def _nsa_proj(x, g, w_in, tables, tm):
    B, S, D = x.shape
    G, hd = NSA_GROUPS, HEAD_DIM
    kvw = G * hd
    nblk = S // SLC_BLOCK
    n_main = D + 6 * kvw
    wg = w_in[:, n_main:].reshape(D, G, NSA_HPG * 3)
    wg = jnp.pad(wg, ((0, 0), (0, 0), (0, LANES - NSA_HPG * 3))).reshape(D, G * LANES)
    w = jnp.concatenate([w_in[:, :n_main], wg], axis=1).astype(BF16)
    nw = w.shape[1]
    c, sp, sm = tables
    vrows = hd + ONES_ROWS
    wtiles = tm // WIN_TILE
    tab_spec = pl.BlockSpec((tm, LANES), lambda b, i: (i, 0))
    kv_spec = lambda width: pl.BlockSpec((1, G, tm, width), lambda b, i: (b, 0, i, 0))
    return pl.pallas_call(
        functools.partial(_nsa_proj_kernel, tm=tm, d=D),
        grid=(B, S // tm),
        in_specs=[
            pl.BlockSpec((1, tm, D), lambda b, i: (b, i, 0)),
            pl.BlockSpec((1, D), lambda b, i: (0, 0)),
            pl.BlockSpec((D, nw), lambda b, i: (0, 0)),
            tab_spec, tab_spec, tab_spec,
        ],
        out_specs=[
            pl.BlockSpec((1, tm, D), lambda b, i: (b, i, 0)),
            pl.BlockSpec((1, tm, 2 * kvw), lambda b, i: (b, i, 0)),
            kv_spec(hd + nblk),
            pl.BlockSpec((1, G, 1, vrows, tm), lambda b, i: (b, 0, i, 0, 0)),
            kv_spec(hd),
            pl.BlockSpec((1, G, wtiles, vrows, WIN_TILE), lambda b, i: (b, 0, i, 0, 0)),
            pl.BlockSpec((1, tm, G * LANES), lambda b, i: (b, i, 0)),
        ],
        out_shape=[
            jax.ShapeDtypeStruct((B, S, D), BF16),
            jax.ShapeDtypeStruct((B, S, 2 * kvw), BF16),
            jax.ShapeDtypeStruct((B, G, S, hd + nblk), BF16),
            jax.ShapeDtypeStruct((B, G, S // tm, vrows, tm), BF16),
            jax.ShapeDtypeStruct((B, G, S, hd), BF16),
            jax.ShapeDtypeStruct((B, G, S // WIN_TILE, vrows, WIN_TILE), BF16),
            jax.ShapeDtypeStruct((B, S, G * LANES), F32),
        ],
        compiler_params=_params(("parallel", "parallel")),
        name="nsa_proj",
    )(x, g.reshape(1, D), w, c, sp, sm)


def _compress_kernel(r_ref, pa_ref, pb_ref, w1a_ref, w1b_ref, w2_ref, c_ref, sp_ref, sm_ref,
                     o_ref, ot_ref):
    r = r_ref[0, 0, 0]
    rows = r.shape[0]
    a = jnp.dot(r, w1a_ref[0], preferred_element_type=F32)
    b = jnp.dot(r, w1b_ref[0], preferred_element_type=F32)
    bias = (jnp.dot(pa_ref[0], w1a_ref[0], preferred_element_type=F32)
            + jnp.dot(pb_ref[0], w1b_ref[0], preferred_element_type=F32))
    hid = a + pltpu.roll(b, rows - 1, 0) + bias[0:1]
    hid = jax.nn.gelu(hid).astype(BF16)
    y = jnp.dot(hid, w2_ref[0], preferred_element_type=F32)
    y = _rope(y, c_ref[0], sp_ref[0], sm_ref[0])
    o_ref[0, 0, 0] = y[:, :HEAD_DIM].astype(BF16)
    ot_ref[0, 0, 0] = y.T[:HEAD_DIM].astype(BF16)


def _compress(kvc, pos, w1, w2):
    B, S, _ = kvc.shape
    G, hd = NSA_GROUPS, HEAD_DIM
    rows = S // CMP_STRIDE
    rw = CMP_STRIDE * hd
    r = kvc.reshape(B, rows, CMP_STRIDE, 2, G, hd).transpose(3, 0, 4, 1, 2, 5).reshape(2, B, G, rows, rw)
    w1 = w1.astype(BF16)
    hidden = w1.shape[2]
    w1a, w1b = w1[:, :rw], w1[:, rw:]
    posr = pos.reshape(2, 2, 1, rw).astype(BF16)
    pa = jnp.broadcast_to(posr[:, 0], (2, 16, rw))
    pb = jnp.broadcast_to(posr[:, 1], (2, 16, rw))
    w2p = jnp.pad(w2, ((0, 0), (0, 0), (0, LANES - hd))).astype(BF16)
    cmp_end = jnp.arange(rows) * CMP_STRIDE + CMP_BLOCK - 1
    c, sp, sm = _rope_tables(cmp_end)
    ident = (jnp.ones_like(c), jnp.zeros_like(sp), jnp.zeros_like(sm))
    c, sp, sm = (jnp.stack([t, u]) for t, u in zip((c, sp, sm), ident))
    per_kind = lambda shape: pl.BlockSpec((1,) + shape, lambda k, b, g: (k,) + (0,) * len(shape))
    return pl.pallas_call(
        _compress_kernel,
        grid=(2, B, G),
        in_specs=[
            pl.BlockSpec((1, 1, 1, rows, rw), lambda k, b, g: (k, b, g, 0, 0)),
            per_kind((16, rw)), per_kind((16, rw)),
            per_kind((rw, hidden)), per_kind((rw, hidden)), per_kind((hidden, LANES)),
            per_kind((rows, LANES)), per_kind((rows, LANES)), per_kind((rows, LANES)),
        ],
        out_specs=[pl.BlockSpec((1, 1, 1, rows, hd), lambda k, b, g: (k, b, g, 0, 0)),
                   pl.BlockSpec((1, 1, 1, hd, rows), lambda k, b, g: (k, b, g, 0, 0))],
        out_shape=[jax.ShapeDtypeStruct((2, B, G, rows, hd), BF16),
                   jax.ShapeDtypeStruct((2, B, G, hd, rows), BF16)],
        compiler_params=_params(("parallel", "parallel", "parallel")),
        name="nsa_compress",
    )(r, pa, pb, w1a, w1b, w2p, c, sp, sm)


def _flash_scratch(chains, tk, r, vrows):
    return [pltpu.VMEM((chains, 2, tk, r), F32), pltpu.VMEM((chains, 2, tk, r), BF16),
            pltpu.VMEM((chains, 1, r), F32), pltpu.VMEM((chains, 2, 1, r), F32),
            pltpu.VMEM((chains, vrows, r), F32)]


def _causal_table(tq, tk):
    y = jnp.arange(2 * tk - tq)[:, None] - (tk - tq)
    return jnp.where(y <= jnp.arange(tq)[None, :], 0.0, NEG_INF).astype(F32)


def _tile_lanes(bias, s):
    tq = bias.shape[1]
    return jnp.concatenate(
        [s[:, x:x + tq] + bias for x in range(0, s.shape[1], tq)], axis=1)


def _flash_causal_t(qats, k_at, vt_at, causal_ref, t0, tq, tk, scratch):
    s_ref, p_ref, m_ref, al_ref, acc_ref = scratch
    chains = range(len(qats))
    nfull = t0 // tk
    n = nfull + 1
    tile = lambda u: jnp.where(u == 0, nfull, jnp.maximum(u - 1, 0))

    def scores(u, slot):
        for c in chains:
            s_ref[c, slot] = jnp.dot(k_at(c, tile(u)), qats[c], preferred_element_type=F32)

    def probs(slot, masked=False):
        for c in chains:
            s = s_ref[c, slot]
            if masked:
                start = pl.multiple_of((tk - tq) - (t0 - nfull * tk), tq)
                s = _tile_lanes(causal_ref[pl.ds(start, tk), :], s)
            m = m_ref[c]
            m_new = jnp.maximum(m, jnp.max(s, axis=0, keepdims=True))
            m_ref[c] = m_new
            al_ref[c, slot] = jnp.exp2(m - m_new)
            p_ref[c, slot] = jnp.exp2(s - m_new).astype(BF16)

    def accum(u, slot):
        for c in chains:
            acc_ref[c] = al_ref[c, slot] * acc_ref[c] + jnp.dot(
                vt_at(c, tile(u)), p_ref[c, slot], preferred_element_type=F32)

    def stage(u, a):
        scores(u + 2, a)
        probs(1 - a)
        accum(u, a)

    m_ref[...] = jnp.full(m_ref.shape, NEG_INF, F32)
    acc_ref[...] = jnp.zeros(acc_ref.shape, F32)
    scores(0, 0)
    probs(0, masked=True)
    scores(1, 1)

    pairs = jnp.maximum(n - 2, 0) // 2

    def body(k, carry):
        stage(2 * k, 0)
        stage(2 * k + 1, 1)
        return carry

    lax.fori_loop(0, pairs, body, 0)

    @pl.when(n == 1)
    def _():
        accum(0, 0)

    @pl.when((n >= 2) & (n % 2 == 0))
    def _():
        accum(n - 2, 0)
        probs(1)
        accum(n - 1, 1)

    @pl.when((n >= 3) & (n % 2 == 1))
    def _():
        stage(n - 3, 0)
        accum(n - 2, 1)
        probs(0)
        accum(n - 1, 0)


def _block_penalty(imp, t0, tq, topk):
    ns = imp.shape[0]
    jblk = lax.broadcasted_iota(jnp.int32, (ns, tq), 0)
    tcol = t0 + lax.broadcasted_iota(jnp.int32, (ns, tq), 1)
    cur = tcol // SLC_BLOCK
    forced = (jblk == 0) | (jblk == cur) | (jblk == cur - 1)
    future = jblk * SLC_BLOCK > tcol
    val = jnp.where(forced, BIG, jnp.where(future, -BIG, imp))
    sub = 8
    vals = [val[a:a + sub] for a in range(0, ns, sub)]
    ranks = [jnp.zeros((sub, tq), F32) for _ in vals]
    row = lax.broadcasted_iota(jnp.int32, (sub, tq), 0)
    for i in range(ns):
        vi = jnp.broadcast_to(val[i:i + 1, :], (sub, tq))
        for a, va in enumerate(vals):
            if a * sub > i:
                hit = jnp.where(vi >= va, 1.0, 0.0)
            elif a * sub + sub - 1 < i:
                hit = jnp.where(vi > va, 1.0, 0.0)
            else:
                hit = jnp.where(row > i - a * sub, jnp.where(vi >= va, 1.0, 0.0),
                                jnp.where(vi > va, 1.0, 0.0))
            ranks[a] = ranks[a] + hit
    return jnp.concatenate([jnp.where(r < topk, 0.0, NEG_INF) for r in ranks], axis=0)


def _nsa_attn_kernel(q_ref, gate_ref, kc_ref, vct_ref, selt_ref, ks_ref, vst_ref, kw_ref, vwt_ref,
                     cmask_ref, wmask_ref, causal_ref, o_ref, pre_ref, *scratch, tq, tk, groups, topk):
    P, hd = NSA_HPG, HEAD_DIM
    R = P * tq
    qw = P * hd
    t0 = pl.program_id(2) * tq
    selt = selt_ref[...]
    ncp = selt.shape[1]
    wk = WINDOW + tq
    wstart = pl.multiple_of(jnp.maximum(t0 - WINDOW, 0), tq)
    gate_t = lambda c: gate_ref[0, :, c * LANES:(c + 1) * LANES].T
    cols = lambda p: slice(p * tq, (p + 1) * tq)

    qats = []
    for c in range(groups):
        qt = q_ref[0, :, c * qw:(c + 1) * qw].astype(F32).T
        q_t = jnp.concatenate([qt[p * hd:(p + 1) * hd] for p in range(P)], axis=1).astype(BF16)
        gt = gate_t(c)

        s = jnp.dot(kc_ref[0, 0, c], q_t, preferred_element_type=F32)
        s = _tile_lanes(cmask_ref[pl.ds(pl.multiple_of(ncp - t0 // CMP_STRIDE, 8), ncp), :], s)
        m_c = jnp.max(s, axis=0, keepdims=True)
        p_c = jnp.exp2(s - m_c)
        l_c = jnp.sum(p_c, axis=0, keepdims=True)
        p_c = p_c * jnp.where(m_c > 0.5 * NEG_INF, 1.0 / l_c, 0.0)
        o_c = jnp.dot(vct_ref[0, 0, c], p_c.astype(BF16), preferred_element_type=F32)

        rem = p_c[:, cols(0)]
        for p in range(1, P):
            rem = rem + p_c[:, cols(p)]
        imp = jnp.zeros((selt.shape[0], tq), F32)
        for _ in range(3):
            part = rem.astype(BF16)
            imp = imp + jnp.dot(selt, part, preferred_element_type=F32)
            rem = rem - part.astype(F32)
        pen = _block_penalty(imp, t0, tq, topk).astype(BF16)
        qats.append(jnp.concatenate([q_t, jnp.concatenate([pen] * P, axis=1)], axis=0))

        s = jnp.dot(kw_ref[0, c, pl.ds(wstart, wk), :], q_t, preferred_element_type=F32)
        s = _tile_lanes(wmask_ref[pl.ds(pl.multiple_of(WINDOW - (t0 - wstart), tq), wk), :], s)
        p_w = jnp.exp2(s - jnp.max(s, axis=0, keepdims=True)).astype(BF16)
        wt0 = wstart // WIN_TILE
        acc_w = jnp.zeros((hd + ONES_ROWS, R), F32)
        for x in range(wk // WIN_TILE):
            acc_w = acc_w + jnp.dot(vwt_ref[0, c, wt0 + x], p_w[x * WIN_TILE:(x + 1) * WIN_TILE],
                                    preferred_element_type=F32)
        o_w = acc_w[:hd] / acc_w[hd:hd + 1]

        for p in range(P):
            pre_ref[c, p * hd:(p + 1) * hd, :] = (gt[3 * p:3 * p + 1] * o_c[:, cols(p)]
                                                  + gt[3 * p + 2:3 * p + 3] * o_w[:, cols(p)])

    _flash_causal_t(
        qats,
        lambda c, j: ks_ref[0, c, pl.ds(pl.multiple_of(j * tk, tk), tk), :],
        lambda c, j: vst_ref[0, c, j],
        causal_ref, t0, tq, tk, scratch)

    acc_ref = scratch[-1]
    for c in range(groups):
        acc = acc_ref[c]
        o_s = acc[:hd] / acc[hd:hd + 1]
        gt = gate_t(c)
        out_t = jnp.concatenate(
            [pre_ref[c, p * hd:(p + 1) * hd, :] + gt[3 * p + 1:3 * p + 2] * o_s[:, cols(p)]
             for p in range(P)], axis=0)
        o_ref[0, :, c * qw:(c + 1) * qw] = out_t.T.astype(BF16)


def _nsa_attn(q, gates, kvc_c, vct, ks, vst, kw, vwt, tq, tk):
    B, S, D = q.shape
    G, P, hd = NSA_GROUPS, NSA_HPG, HEAD_DIM
    groups = 2
    ncp = kvc_c.shape[3]
    ns = S // SLC_BLOCK
    nt = S // tk
    nwt = S // WIN_TILE
    topk = min(SLC_TOPK, ns)
    wk = WINDOW + tq
    tl = jnp.arange(tq)[None, :]
    y = jnp.arange(2 * ncp)[:, None] - ncp
    cmask = jnp.where(y * CMP_STRIDE + CMP_BLOCK - 1 <= tl, 0.0, NEG_INF).astype(F32)
    y = jnp.arange(WINDOW + wk)[:, None]
    wmask = jnp.where((y - WINDOW <= tl) & (y > tl), 0.0, NEG_INF).astype(F32)
    causal = _causal_table(tq, tk)
    whole = lambda a: pl.BlockSpec(a.shape, lambda b, g, i: (0,) * a.ndim)
    n = jnp.arange(ncp)
    c_start, c_end = n * CMP_STRIDE, n * CMP_STRIDE + CMP_BLOCK - 1
    sj = jnp.arange(ns) * SLC_BLOCK
    selt = ((c_end[None, :] >= sj[:, None]) & (c_start[None, :] <= sj[:, None] + SLC_BLOCK - 1)
            & (n[None, :] < ncp - 1)).astype(BF16)
    vrows = hd + ONES_ROWS
    tok_spec = lambda width: pl.BlockSpec((1, tq, groups * width), lambda b, g, i: (b, i, g))
    return pl.pallas_call(
        functools.partial(_nsa_attn_kernel, tq=tq, tk=tk, groups=groups, topk=topk),
        grid=(B, G // groups, S // tq),
        in_specs=[
            tok_spec(P * hd),
            tok_spec(LANES),
            pl.BlockSpec((1, 1, groups, ncp, hd), lambda b, g, i: (0, b, g, 0, 0)),
            pl.BlockSpec((1, 1, groups, hd, ncp), lambda b, g, i: (1, b, g, 0, 0)),
            whole(selt),
            pl.BlockSpec((1, groups, S, hd + ns), lambda b, g, i: (b, g, 0, 0)),
            pl.BlockSpec((1, groups, nt, vrows, tk), lambda b, g, i: (b, g, 0, 0, 0)),
            pl.BlockSpec((1, groups, S, hd), lambda b, g, i: (b, g, 0, 0)),
            pl.BlockSpec((1, groups, nwt, vrows, WIN_TILE), lambda b, g, i: (b, g, 0, 0, 0)),
            whole(cmask), whole(wmask), whole(causal),
        ],
        out_specs=tok_spec(P * hd),
        out_shape=jax.ShapeDtypeStruct((B, S, D), BF16),
        scratch_shapes=[pltpu.VMEM((groups, P * hd, tq), F32)]
        + _flash_scratch(groups, tk, P * tq, vrows),
        compiler_params=_params(("parallel", "parallel", "arbitrary")),
        name="nsa_attn",
    )(q, gates, kvc_c, vct, selt, ks, vst, kw, vwt, cmask, wmask, causal)


def _diff_proj_kernel(x_ref, g_ref, w_ref, c_ref, sp_ref, sm_ref, q_ref, k_ref, vt_ref, *, d):
    h = _rms(x_ref[0], g_ref[...]).astype(BF16)
    c, sp, sm = c_ref[...], sp_ref[...], sm_ref[...]
    q = jnp.dot(h, w_ref[:, 0:d], preferred_element_type=F32)
    q_ref[0] = (_rope_wide(q, c, sp, sm) * (HEAD_DIM ** -0.5 * LOG2E)).astype(BF16)
    k = jnp.dot(h, w_ref[:, d:2 * d], preferred_element_type=F32)
    k_ref[0] = _rope_wide(k, c, sp, sm).astype(BF16)
    hw = 2 * HEAD_DIM
    v_t = jnp.dot(h, w_ref[:, 2 * d:3 * d], preferred_element_type=F32).T.astype(BF16)
    ones_rows = _ones_rows(v_t.shape[1])
    for hh in range(d // hw):
        vt_ref[0, hh, 0, :hw, :] = v_t[hh * hw:(hh + 1) * hw]
        vt_ref[0, hh, 0, hw:, :] = ones_rows


def _diff_proj(x, g, w_in, tables, tm):
    B, S, D = x.shape
    hw = 2 * HEAD_DIM
    H = D // hw
    c, sp, sm = tables
    tab_spec = pl.BlockSpec((tm, LANES), lambda b, i: (i, 0))
    tok_spec = pl.BlockSpec((1, tm, D), lambda b, i: (b, i, 0))
    return pl.pallas_call(
        functools.partial(_diff_proj_kernel, d=D),
        grid=(B, S // tm),
        in_specs=[tok_spec, pl.BlockSpec((1, D), lambda b, i: (0, 0)),
                  pl.BlockSpec((D, 3 * D), lambda b, i: (0, 0)), tab_spec, tab_spec, tab_spec],
        out_specs=[tok_spec, tok_spec,
                   pl.BlockSpec((1, H, 1, hw + ONES_ROWS, tm), lambda b, i: (b, 0, i, 0, 0))],
        out_shape=[jax.ShapeDtypeStruct((B, S, D), BF16), jax.ShapeDtypeStruct((B, S, D), BF16),
                   jax.ShapeDtypeStruct((B, H, S // tm, hw + ONES_ROWS, tm), BF16)],
        compiler_params=_params(("parallel", "parallel")),
        name="diff_proj",
    )(x, g.reshape(1, D), w_in.astype(BF16), c, sp, sm)


def _diff_attn_kernel(lq1_ref, lk1_ref, lq2_ref, lk2_ref, sg_ref, q_ref, k_ref, vt_ref, causal_ref,
                      o_ref, *scratch, tq, tk, heads, lambda_init):
    hw = 2 * HEAD_DIM
    t0 = pl.program_id(2) * tq
    lam = (jnp.exp(jnp.sum(lq1_ref[...] * lk1_ref[...], axis=1, keepdims=True))
           - jnp.exp(jnp.sum(lq2_ref[...] * lk2_ref[...], axis=1, keepdims=True)) + lambda_init)
    qats = []
    for c in range(heads):
        qt = q_ref[0, :, c * hw:(c + 1) * hw].astype(F32).T
        row = lax.broadcasted_iota(jnp.int32, qt.shape, 0)
        qats.append(jnp.concatenate([jnp.where(row < HEAD_DIM, qt, 0.0),
                                     jnp.where(row >= HEAD_DIM, qt, 0.0)], axis=1).astype(BF16))
    _flash_causal_t(
        qats,
        lambda c, j: k_ref[0, pl.ds(pl.multiple_of(j * tk, tk), tk), c * hw:(c + 1) * hw],
        lambda c, j: vt_ref[0, c, j],
        causal_ref, t0, tq, tk, scratch)
    acc_ref = scratch[-1]
    for c in range(heads):
        acc = acc_ref[c]
        o = acc[:hw] / acc[hw:hw + 1]
        o = o[:, :tq] - lam * o[:, tq:]
        o = o * lax.rsqrt(jnp.mean(o * o, axis=0, keepdims=True) + EPS) * sg_ref[...]
        o_ref[0, :, c * hw:(c + 1) * hw] = (o * (1.0 - lambda_init)).T.astype(BF16)


def _diff_attn(q, k, vt, lq1, lk1, lq2, lk2, subln_g, lambda_init, tq, tk, heads):
    B, S, D = q.shape
    hw = 2 * HEAD_DIM
    H = D // hw
    nt = S // tk
    causal = _causal_table(tq, tk)
    vec = lambda a: a.reshape(1, -1).astype(F32)
    vec_spec = lambda n: pl.BlockSpec((1, n), lambda b, h, i: (0, 0))
    q_spec = pl.BlockSpec((1, tq, heads * hw), lambda b, h, i: (b, i, h))
    return pl.pallas_call(
        functools.partial(_diff_attn_kernel, tq=tq, tk=tk, heads=heads, lambda_init=lambda_init),
        grid=(B, H // heads, S // tq),
        in_specs=[vec_spec(HEAD_DIM)] * 4 + [
            pl.BlockSpec((hw, 1), lambda b, h, i: (0, 0)),
            q_spec,
            pl.BlockSpec((1, S, heads * hw), lambda b, h, i: (b, 0, h)),
            pl.BlockSpec((1, heads, nt, hw + ONES_ROWS, tk), lambda b, h, i: (b, h, 0, 0, 0)),
            pl.BlockSpec(causal.shape, lambda b, h, i: (0, 0)),
        ],
        out_specs=q_spec,
        out_shape=jax.ShapeDtypeStruct((B, S, D), BF16),
        scratch_shapes=_flash_scratch(heads, tk, 2 * tq, hw + ONES_ROWS),
        compiler_params=_params(("parallel", "parallel", "arbitrary")),
        name="diff_attn",
    )(vec(lq1), vec(lk1), vec(lq2), vec(lk2), subln_g.reshape(hw, 1).astype(F32), q, k, vt, causal)


def _post_kernel(x_ref, a_ref, wo_ref, g_ref, wu_ref, wd_ref, fg_ref, o_ref, *, ff_chunk, final):
    x = x_ref[...] + jnp.dot(a_ref[...], wo_ref[...], preferred_element_type=F32)
    h = _rms(x, g_ref[...]).astype(BF16)
    y = x
    for lo in range(0, wu_ref.shape[1], ff_chunk):
        u = jnp.maximum(jnp.dot(h, wu_ref[:, lo:lo + ff_chunk], preferred_element_type=F32), 0.0)
        y = y + jnp.dot((u * u).astype(BF16), wd_ref[lo:lo + ff_chunk, :], preferred_element_type=F32)
    if final:
        y = _rms(y, fg_ref[...])
    o_ref[...] = y


def _post(x, a, w_out, g, w_up, w_down, final_g, tm, ff_chunk):
    B, S, D = x.shape
    T = B * S
    dff = w_up.shape[1]
    final = final_g is not None
    fg = (final_g if final else jnp.ones((D,), F32)).reshape(1, D)
    const = lambda shape: pl.BlockSpec(shape, lambda i: (0, 0), pipeline_mode=pl.Buffered(1))
    tok = pl.BlockSpec((tm, D), lambda i: (i, 0))
    out = pl.pallas_call(
        functools.partial(_post_kernel, ff_chunk=ff_chunk, final=final),
        grid=(T // tm,),
        in_specs=[tok, tok, const((D, D)), const((1, D)), const((D, dff)), const((dff, D)),
                  const((1, D))],
        out_specs=tok,
        out_shape=jax.ShapeDtypeStruct((T, D), F32),
        compiler_params=_params(("parallel",)),
        name="post_mlp",
    )(x.reshape(T, D), a.reshape(T, D), w_out.astype(BF16), g.reshape(1, D),
      w_up.astype(BF16), w_down.astype(BF16), fg)
    return out.reshape(B, S, D)


def kernel(x, attn_norm_g, mlp_norm_g, nsa_w_in, nsa_ck_pos, nsa_ck_w1, nsa_ck_w2, nsa_cv_pos, nsa_cv_w1, nsa_cv_w2, nsa_w_out, diff_w_in, diff_lq1, diff_lk1, diff_lq2, diff_lk2, diff_subln_g, diff_w_out, mlp_w_up, mlp_w_down, final_norm_g):
    B, S, D = x.shape
    depth = attn_norm_g.shape[0]
    tables = _rope_tables(jnp.arange(S))
    tm = min(512, S)
    tk = tm
    for i in range(depth):
        j = i // 2
        if i % 2 == 0:
            q, kvc, ks, vst, kw, vwt, gates = _nsa_proj(x, attn_norm_g[i], nsa_w_in[j], tables, tm)
            kvc_c, vct = _compress(kvc,
                                   jnp.stack([nsa_ck_pos[j], nsa_cv_pos[j]]),
                                   jnp.stack([nsa_ck_w1[j], nsa_cv_w1[j]]),
                                   jnp.stack([nsa_ck_w2[j], nsa_cv_w2[j]]))
            a = _nsa_attn(q, gates, kvc_c, vct, ks, vst, kw, vwt, tq=256, tk=tk)
            w_out = nsa_w_out[j]
        else:
            lambda_init = 0.8 - 0.6 * math.exp(-0.3 * i)
            q, k, vt = _diff_proj(x, attn_norm_g[i], diff_w_in[j], tables, tm)
            a = _diff_attn(q, k, vt, diff_lq1[j], diff_lk1[j], diff_lq2[j], diff_lk2[j],
                           diff_subln_g[j], lambda_init, tq=tk, tk=tk, heads=2)
            w_out = diff_w_out[j]
        x = _post(x, a, w_out, mlp_norm_g[i], mlp_w_up[i], mlp_w_down[i],
                  final_norm_g if i == depth - 1 else None, tm, ff_chunk=1024)
    return x
```

```python
import functools
import math

import jax
import jax.numpy as jnp
from jax import lax
from jax.experimental import pallas as pl
from jax.experimental.pallas import tpu as pltpu

F32 = jnp.float32
BF16 = jnp.bfloat16

EPS = 1e-6
ROPE_THETA = 500000.0
HEAD_DIM = 64
ROT_HALF = HEAD_DIM // 8
LANES = 128
NEG_INF = -1e30
BIG = 1e9
LOG2E = math.log2(math.e)
ONES_ROWS = 16

NSA_GROUPS = 4
NSA_HPG = 4
CMP_BLOCK = 32
CMP_STRIDE = 16
SLC_BLOCK = 64
SLC_TOPK = 16
WINDOW = 512
WIN_TILE = 128

VMEM_LIMIT = 56 * 1024 * 1024


def _params(sem):
    return pltpu.CompilerParams(dimension_semantics=sem, vmem_limit_bytes=VMEM_LIMIT)


def _rms(x, g):
    return x * lax.rsqrt(jnp.mean(x * x, axis=-1, keepdims=True) + EPS) * g


def _rope_tables(positions):
    n = positions.shape[0]
    rot = 2 * ROT_HALF
    inv = 1.0 / (ROPE_THETA ** (jnp.arange(0, rot, 2, dtype=F32) / rot))
    ang = positions.astype(F32)[:, None] * inv[None, :]
    cos, sin = jnp.cos(ang), jnp.sin(ang)
    rest = HEAD_DIM - rot
    c = jnp.concatenate([cos, cos, jnp.ones((n, rest), F32)], axis=1)
    sp = jnp.concatenate([jnp.zeros((n, ROT_HALF), F32), sin, jnp.zeros((n, rest), F32)], axis=1)
    sm = jnp.concatenate([-sin, jnp.zeros((n, HEAD_DIM - ROT_HALF), F32)], axis=1)
    reps = LANES // HEAD_DIM
    return tuple(jnp.tile(t, (1, reps)) for t in (c, sp, sm))


def _rope(y, c, sp, sm):
    return (y * c + pltpu.roll(y, ROT_HALF, 1) * sp
            + pltpu.roll(y, LANES - ROT_HALF, 1) * sm)


def _rope_wide(y, c, sp, sm):
    n = y.shape[1] // LANES
    return jnp.concatenate(
        [_rope(y[:, k * LANES:(k + 1) * LANES], c, sp, sm) for k in range(n)], axis=1)


def _ones_rows(width):
    row = lax.broadcasted_iota(jnp.int32, (ONES_ROWS, width), 0)
    return jnp.where(row == 0, 1.0, 0.0).astype(BF16)


def _nsa_proj_kernel(x_ref, g_ref, w_ref, c_ref, sp_ref, sm_ref,
                     q_ref, kvc_ref, ks_ref, vs_ref, kw_ref, vw_ref, gate_ref, *, tm, d):
    i = pl.program_id(1)
    h = _rms(x_ref[0], g_ref[...]).astype(BF16)
    c, sp, sm = c_ref[...], sp_ref[...], sm_ref[...]
    kvw = NSA_GROUPS * HEAD_DIM

    def proj(lo, width):
        return jnp.dot(h, w_ref[:, lo:lo + width], preferred_element_type=F32)

    q = proj(0, d)
    q_ref[0] = (_rope_wide(q, c, sp, sm) * (HEAD_DIM ** -0.5 * LOG2E)).astype(BF16)
    kvc_ref[0] = proj(d, 2 * kvw).astype(BF16)

    ksl = _rope_wide(proj(d + 2 * kvw, kvw), c, sp, sm)
    vsl = proj(d + 3 * kvw, kvw)
    kwn = _rope_wide(proj(d + 4 * kvw, kvw), c, sp, sm)
    vwn = proj(d + 5 * kvw, kvw)
    nblk = ks_ref.shape[3] - HEAD_DIM
    blk = (lax.broadcasted_iota(jnp.int32, (tm, nblk), 0) // SLC_BLOCK) + i * (tm // SLC_BLOCK)
    onehot = jnp.where(blk == lax.broadcasted_iota(jnp.int32, (tm, nblk), 1), 1.0, 0.0)
    vsl_t, vwn_t = vsl.T.astype(BF16), vwn.T.astype(BF16)
    ones_rows = _ones_rows(tm)
    for g in range(NSA_GROUPS):
        sl = slice(g * HEAD_DIM, (g + 1) * HEAD_DIM)
        ks_ref[0, g] = jnp.concatenate([ksl[:, sl], onehot], axis=1).astype(BF16)
        kw_ref[0, g] = kwn[:, sl].astype(BF16)
        vs_ref[0, g, 0, :HEAD_DIM, :] = vsl_t[sl]
        vs_ref[0, g, 0, HEAD_DIM:, :] = ones_rows
        for x in range(tm // WIN_TILE):
            vw_ref[0, g, x, :HEAD_DIM, :] = vwn_t[sl, x * WIN_TILE:(x + 1) * WIN_TILE]
            vw_ref[0, g, x, HEAD_DIM:, :] = ones_rows[:, :WIN_TILE]
    gate_ref[0] = jax.nn.sigmoid(proj(d + 6 * kvw, NSA_GROUPS * LANES))


def _nsa_proj(x, g, w_in, tables, tm):
    B, S, D = x.shape
    G, hd = NSA_GROUPS, HEAD_DIM
    kvw = G * hd
    nblk = S // SLC_BLOCK
    n_main = D + 6 * kvw
    wg = w_in[:, n_main:].reshape(D, G, NSA_HPG * 3)
    wg = jnp.pad(wg, ((0, 0), (0, 0), (0, LANES - NSA_HPG * 3))).reshape(D, G * LANES)
    w = jnp.concatenate([w_in[:, :n_main], wg], axis=1).astype(BF16)
    nw = w.shape[1]
    c, sp, sm = tables
    vrows = hd + ONES_ROWS
    wtiles = tm // WIN_TILE
    tab_spec = pl.BlockSpec((tm, LANES), lambda b, i: (i, 0))
    kv_spec = lambda width: pl.BlockSpec((1, G, tm, width), lambda b, i: (b, 0, i, 0))
    return pl.pallas_call(
        functools.partial(_nsa_proj_kernel, tm=tm, d=D),
        grid=(B, S // tm),
        in_specs=[
            pl.BlockSpec((1, tm, D), lambda b, i: (b, i, 0)),
            pl.BlockSpec((1, D), lambda b, i: (0, 0)),
            pl.BlockSpec((D, nw), lambda b, i: (0, 0)),
            tab_spec, tab_spec, tab_spec,
        ],
        out_specs=[
            pl.BlockSpec((1, tm, D), lambda b, i: (b, i, 0)),
            pl.BlockSpec((1, tm, 2 * kvw), lambda b, i: (b, i, 0)),
            kv_spec(hd + nblk),
            pl.BlockSpec((1, G, 1, vrows, tm), lambda b, i: (b, 0, i, 0, 0)),
            kv_spec(hd),
            pl.BlockSpec((1, G, wtiles, vrows, WIN_TILE), lambda b, i: (b, 0, i, 0, 0)),
            pl.BlockSpec((1, tm, G * LANES), lambda b, i: (b, i, 0)),
        ],
        out_shape=[
            jax.ShapeDtypeStruct((B, S, D), BF16),
            jax.ShapeDtypeStruct((B, S, 2 * kvw), BF16),
            jax.ShapeDtypeStruct((B, G, S, hd + nblk), BF16),
            jax.ShapeDtypeStruct((B, G, S // tm, vrows, tm), BF16),
            jax.ShapeDtypeStruct((B, G, S, hd), BF16),
            jax.ShapeDtypeStruct((B, G, S // WIN_TILE, vrows, WIN_TILE), BF16),
            jax.ShapeDtypeStruct((B, S, G * LANES), F32),
        ],
        compiler_params=_params(("parallel", "parallel")),
        name="nsa_proj",
    )(x, g.reshape(1, D), w, c, sp, sm)


def _compress_kernel(r_ref, pa_ref, pb_ref, w1a_ref, w1b_ref, w2_ref, c_ref, sp_ref, sm_ref,
                     o_ref, ot_ref):
    r = r_ref[0, 0, 0]
    rows = r.shape[0]
    a = jnp.dot(r, w1a_ref[0], preferred_element_type=F32)
    b = jnp.dot(r, w1b_ref[0], preferred_element_type=F32)
    bias = (jnp.dot(pa_ref[0], w1a_ref[0], preferred_element_type=F32)
            + jnp.dot(pb_ref[0], w1b_ref[0], preferred_element_type=F32))
    hid = a + pltpu.roll(b, rows - 1, 0) + bias[0:1]
    hid = jax.nn.gelu(hid).astype(BF16)
    y = jnp.dot(hid, w2_ref[0], preferred_element_type=F32)
    y = _rope(y, c_ref[0], sp_ref[0], sm_ref[0])
    o_ref[0, 0, 0] = y[:, :HEAD_DIM].astype(BF16)
    ot_ref[0, 0, 0] = y.T[:HEAD_DIM].astype(BF16)


def _compress(kvc, pos, w1, w2):
    B, S, _ = kvc.shape
    G, hd = NSA_GROUPS, HEAD_DIM
    rows = S // CMP_STRIDE
    rw = CMP_STRIDE * hd
    r = kvc.reshape(B, rows, CMP_STRIDE, 2, G, hd).transpose(3, 0, 4, 1, 2, 5).reshape(2, B, G, rows, rw)
    w1 = w1.astype(BF16)
    hidden = w1.shape[2]
    w1a, w1b = w1[:, :rw], w1[:, rw:]
    posr = pos.reshape(2, 2, 1, rw).astype(BF16)
    pa = jnp.broadcast_to(posr[:, 0], (2, 16, rw))
    pb = jnp.broadcast_to(posr[:, 1], (2, 16, rw))
    w2p = jnp.pad(w2, ((0, 0), (0, 0), (0, LANES - hd))).astype(BF16)
    cmp_end = jnp.arange(rows) * CMP_STRIDE + CMP_BLOCK - 1
    c, sp, sm = _rope_tables(cmp_end)
    ident = (jnp.ones_like(c), jnp.zeros_like(sp), jnp.zeros_like(sm))
    c, sp, sm = (jnp.stack([t, u]) for t, u in zip((c, sp, sm), ident))
    per_kind = lambda shape: pl.BlockSpec((1,) + shape, lambda k, b, g: (k,) + (0,) * len(shape))
    return pl.pallas_call(
        _compress_kernel,
        grid=(2, B, G),
        in_specs=[
            pl.BlockSpec((1, 1, 1, rows, rw), lambda k, b, g: (k, b, g, 0, 0)),
            per_kind((16, rw)), per_kind((16, rw)),
            per_kind((rw, hidden)), per_kind((rw, hidden)), per_kind((hidden, LANES)),
            per_kind((rows, LANES)), per_kind((rows, LANES)), per_kind((rows, LANES)),
        ],
        out_specs=[pl.BlockSpec((1, 1, 1, rows, hd), lambda k, b, g: (k, b, g, 0, 0)),
                   pl.BlockSpec((1, 1, 1, hd, rows), lambda k, b, g: (k, b, g, 0, 0))],
        out_shape=[jax.ShapeDtypeStruct((2, B, G, rows, hd), BF16),
                   jax.ShapeDtypeStruct((2, B, G, hd, rows), BF16)],
        compiler_params=_params(("parallel", "parallel", "parallel")),
        name="nsa_compress",
    )(r, pa, pb, w1a, w1b, w2p, c, sp, sm)


def _causal_table(tq, tk):
    y = jnp.arange(2 * tk - tq)[:, None] - (tk - tq)
    return jnp.where(y <= jnp.arange(tq)[None, :], 0.0, NEG_INF).astype(F32)


def _tile_lanes(bias, s):
    tq = bias.shape[1]
    return jnp.concatenate(
        [s[:, x:x + tq] + bias for x in range(0, s.shape[1], tq)], axis=1)


def _flash_pair_scratch(chains, tk, r, vrows):
    return [pltpu.VMEM((chains, 2, tk, r), F32), pltpu.VMEM((chains, 2, tk, r), BF16),
            pltpu.VMEM((chains, 2, 1, r), F32), pltpu.VMEM((chains, 2, 1, r), F32),
            pltpu.VMEM((chains, 2, vrows, r), F32)]


def _flash_pair(qat_ref, k_at, vt_at, causal_ref, ia, ib, tq, tk, nsteps, scratch):
    s_ref, p_ref, al_ref, m_ref, acc_ref = scratch
    chains = range(qat_ref.shape[0])
    nfull = [(i * tq) // tk for i in (ia, ib)]

    def info(t):
        if isinstance(t, int) and t < 2:
            return t, nfull[t], True
        x = jnp.where(t - 2 >= nfull[0], 1, 0)
        return x, t - 2 - x * nfull[0], False

    def scores(t, slot):
        x, tile, diag = info(t)
        for c in chains:
            s = jnp.dot(k_at(c, tile), qat_ref[c, x], preferred_element_type=F32)
            if diag:
                i = (ia, ib)[x]
                start = pl.multiple_of((tk - tq) - (i * tq - nfull[x] * tk), tq)
                s = _tile_lanes(causal_ref[pl.ds(start, tk), :], s)
            s_ref[c, slot] = s

    def probs(t, slot):
        x, _, diag = info(t)
        for c in chains:
            m_new = jnp.max(s_ref[c, slot], axis=0, keepdims=True)
            if not diag:
                m = m_ref[c, x]
                m_new = jnp.maximum(m, m_new)
                al_ref[c, slot] = jnp.exp2(m - m_new)
            m_ref[c, x] = m_new
            p_ref[c, slot] = jnp.exp2(s_ref[c, slot] - m_new).astype(BF16)

    def accum(t, slot):
        x, tile, diag = info(t)
        for c in chains:
            pv = jnp.dot(vt_at(c, tile), p_ref[c, slot], preferred_element_type=F32)
            acc_ref[c, x] = pv if diag else al_ref[c, slot] * acc_ref[c, x] + pv

    def stage(t, a):
        scores(t + 2, a)
        probs(t + 1, 1 - a)
        accum(t, a)

    scores(0, 0)
    probs(0, 0)
    scores(1, 1)
    stage(0, 0)
    stage(1, 1)
    steady = nsteps - 4
    pairs = steady // 2

    def body(k, carry):
        stage(2 * k + 2, 0)
        stage(2 * k + 3, 1)
        return carry

    lax.fori_loop(0, pairs, body, 0)
    a = 0
    if steady % 2:
        stage(nsteps - 3, 0)
        a = 1
    accum(nsteps - 2, a)
    probs(nsteps - 1, 1 - a)
    accum(nsteps - 1, 1 - a)


def _block_penalty(imp, t0, tq, topk):
    ns = imp.shape[0]
    jblk = lax.broadcasted_iota(jnp.int32, (ns, tq), 0)
    tcol = t0 + lax.broadcasted_iota(jnp.int32, (ns, tq), 1)
    cur = tcol // SLC_BLOCK
    forced = (jblk == 0) | (jblk == cur) | (jblk == cur - 1)
    future = jblk * SLC_BLOCK > tcol
    val = jnp.where(forced, BIG, jnp.where(future, -BIG, imp))
    sub = 8
    vals = [val[a:a + sub] for a in range(0, ns, sub)]
    ranks = [jnp.zeros((sub, tq), F32) for _ in vals]
    row = lax.broadcasted_iota(jnp.int32, (sub, tq), 0)
    for i in range(ns):
        vi = jnp.broadcast_to(val[i:i + 1, :], (sub, tq))
        for a, va in enumerate(vals):
            if a * sub > i:
                hit = jnp.where(vi >= va, 1.0, 0.0)
            elif a * sub + sub - 1 < i:
                hit = jnp.where(vi > va, 1.0, 0.0)
            else:
                hit = jnp.where(row > i - a * sub, jnp.where(vi >= va, 1.0, 0.0),
                                jnp.where(vi > va, 1.0, 0.0))
            ranks[a] = ranks[a] + hit
    return jnp.concatenate([jnp.where(r < topk, 0.0, NEG_INF) for r in ranks], axis=0)


def _nsa_attn_kernel(qa_ref, qb_ref, ga_ref, gb_ref, kc_ref, vct_ref, selt_ref, ks_ref, vst_ref, kw_ref,
                     vwt_ref, cmask_ref, wmask_ref, causal_ref, o_ref, qat_ref, pre_ref, *scratch,
                     tq, tk, groups, topk):
    P, hd = NSA_HPG, HEAD_DIM
    R = P * tq
    qw = P * hd
    nq = o_ref.shape[1] // tq
    ia = pl.program_id(2)
    pair = ((0, ia, qa_ref, ga_ref), (1, nq - 1 - ia, qb_ref, gb_ref))
    selt = selt_ref[...]
    ncp = selt.shape[1]
    wk = WINDOW + tq
    gate_t = lambda g_ref, c: g_ref[0, :, c * LANES:(c + 1) * LANES].T
    cols = lambda p: slice(p * tq, (p + 1) * tq)

    for x, i, q_ref, g_ref in pair:
        t0 = i * tq
        wstart = pl.multiple_of(jnp.maximum(t0 - WINDOW, 0), tq)
        for c in range(groups):
            qt = q_ref[0, :, c * qw:(c + 1) * qw].astype(F32).T
            q_t = jnp.concatenate([qt[p * hd:(p + 1) * hd] for p in range(P)], axis=1).astype(BF16)
            gt = gate_t(g_ref, c)

            s = jnp.dot(kc_ref[0, 0, c], q_t, preferred_element_type=F32)
            s = _tile_lanes(cmask_ref[pl.ds(pl.multiple_of(ncp - t0 // CMP_STRIDE, 8), ncp), :], s)
            m_c = jnp.max(s, axis=0, keepdims=True)
            p_c = jnp.exp2(s - m_c)
            l_c = jnp.sum(p_c, axis=0, keepdims=True)
            p_c = p_c * jnp.where(m_c > 0.5 * NEG_INF, 1.0 / l_c, 0.0)
            o_c = jnp.dot(vct_ref[0, 0, c], p_c.astype(BF16), preferred_element_type=F32)

            rem = p_c[:, cols(0)]
            for p in range(1, P):
                rem = rem + p_c[:, cols(p)]
            imp = jnp.zeros((selt.shape[0], tq), F32)
            for _ in range(3):
                part = rem.astype(BF16)
                imp = imp + jnp.dot(selt, part, preferred_element_type=F32)
                rem = rem - part.astype(F32)
            pen = _block_penalty(imp, t0, tq, topk).astype(BF16)
            qat_ref[c, x] = jnp.concatenate([q_t, jnp.concatenate([pen] * P, axis=1)], axis=0)

            s = jnp.dot(kw_ref[0, c, pl.ds(wstart, wk), :], q_t, preferred_element_type=F32)
            s = _tile_lanes(wmask_ref[pl.ds(pl.multiple_of(WINDOW - (t0 - wstart), tq), wk), :], s)
            p_w = jnp.exp2(s - jnp.max(s, axis=0, keepdims=True)).astype(BF16)
            wt0 = wstart // WIN_TILE
            acc_w = jnp.zeros((hd + ONES_ROWS, R), F32)
            for y in range(wk // WIN_TILE):
                acc_w = acc_w + jnp.dot(vwt_ref[0, c, wt0 + y], p_w[y * WIN_TILE:(y + 1) * WIN_TILE],
                                        preferred_element_type=F32)
            o_w = acc_w[:hd] / acc_w[hd:hd + 1]

            for p in range(P):
                pre_ref[c, x, p * hd:(p + 1) * hd, :] = (gt[3 * p:3 * p + 1] * o_c[:, cols(p)]
                                                         + gt[3 * p + 2:3 * p + 3] * o_w[:, cols(p)])

    _flash_pair(
        qat_ref,
        lambda c, j: ks_ref[0, c, pl.ds(pl.multiple_of(j * tk, tk), tk), :],
        lambda c, j: vst_ref[0, c, j],
        causal_ref, ia, nq - 1 - ia, tq, tk, (nq * tq) // tk + 1, scratch)

    acc_ref = scratch[-1]
    for x, i, _, g_ref in pair:
        for c in range(groups):
            acc = acc_ref[c, x]
            o_s = acc[:hd] / acc[hd:hd + 1]
            gt = gate_t(g_ref, c)
            out_t = jnp.concatenate(
                [pre_ref[c, x, p * hd:(p + 1) * hd, :] + gt[3 * p + 1:3 * p + 2] * o_s[:, cols(p)]
                 for p in range(P)], axis=0)
            o_ref[0, pl.ds(pl.multiple_of(i * tq, tq), tq), c * qw:(c + 1) * qw] = out_t.T.astype(BF16)


def _nsa_attn(q, gates, kvc_c, vct, ks, vst, kw, vwt, tq, tk):
    B, S, D = q.shape
    G, P, hd = NSA_GROUPS, NSA_HPG, HEAD_DIM
    groups = 2
    ncp = kvc_c.shape[3]
    ns = S // SLC_BLOCK
    nt = S // tk
    nwt = S // WIN_TILE
    topk = min(SLC_TOPK, ns)
    wk = WINDOW + tq
    tl = jnp.arange(tq)[None, :]
    y = jnp.arange(2 * ncp)[:, None] - ncp
    cmask = jnp.where(y * CMP_STRIDE + CMP_BLOCK - 1 <= tl, 0.0, NEG_INF).astype(F32)
    y = jnp.arange(WINDOW + wk)[:, None]
    wmask = jnp.where((y - WINDOW <= tl) & (y > tl), 0.0, NEG_INF).astype(F32)
    causal = _causal_table(tq, tk)
    whole = lambda a: pl.BlockSpec(a.shape, lambda b, g, i: (0,) * a.ndim)
    n = jnp.arange(ncp)
    c_start, c_end = n * CMP_STRIDE, n * CMP_STRIDE + CMP_BLOCK - 1
    sj = jnp.arange(ns) * SLC_BLOCK
    selt = ((c_end[None, :] >= sj[:, None]) & (c_start[None, :] <= sj[:, None] + SLC_BLOCK - 1)
            & (n[None, :] < ncp - 1)).astype(BF16)
    vrows = hd + ONES_ROWS
    nq = S // tq
    tile_a = lambda width: pl.BlockSpec((1, tq, groups * width), lambda b, g, i: (b, i, g))
    tile_b = lambda width: pl.BlockSpec((1, tq, groups * width), lambda b, g, i: (b, nq - 1 - i, g))
    return pl.pallas_call(
        functools.partial(_nsa_attn_kernel, tq=tq, tk=tk, groups=groups, topk=topk),
        grid=(B, G // groups, nq // 2),
        in_specs=[
            tile_a(P * hd), tile_b(P * hd),
            tile_a(LANES), tile_b(LANES),
            pl.BlockSpec((1, 1, groups, ncp, hd), lambda b, g, i: (0, b, g, 0, 0)),
            pl.BlockSpec((1, 1, groups, hd, ncp), lambda b, g, i: (1, b, g, 0, 0)),
            whole(selt),
            pl.BlockSpec((1, groups, S, hd + ns), lambda b, g, i: (b, g, 0, 0)),
            pl.BlockSpec((1, groups, nt, vrows, tk), lambda b, g, i: (b, g, 0, 0, 0)),
            pl.BlockSpec((1, groups, S, hd), lambda b, g, i: (b, g, 0, 0)),
            pl.BlockSpec((1, groups, nwt, vrows, WIN_TILE), lambda b, g, i: (b, g, 0, 0, 0)),
            whole(cmask), whole(wmask), whole(causal),
        ],
        out_specs=pl.BlockSpec((1, S, groups * P * hd), lambda b, g, i: (b, 0, g)),
        out_shape=jax.ShapeDtypeStruct((B, S, D), BF16),
        scratch_shapes=[pltpu.VMEM((groups, 2, hd + ns, P * tq), BF16),
                        pltpu.VMEM((groups, 2, P * hd, tq), F32)]
        + _flash_pair_scratch(groups, tk, P * tq, vrows),
        compiler_params=_params(("parallel", "parallel", "arbitrary")),
        name="nsa_attn",
    )(q, q, gates, gates, kvc_c, vct, selt, ks, vst, kw, vwt, cmask, wmask, causal)


def _diff_proj_kernel(x_ref, g_ref, w_ref, c_ref, sp_ref, sm_ref, q_ref, k_ref, vt_ref, *, d):
    h = _rms(x_ref[0], g_ref[...]).astype(BF16)
    c, sp, sm = c_ref[...], sp_ref[...], sm_ref[...]
    q = jnp.dot(h, w_ref[:, 0:d], preferred_element_type=F32)
    q_ref[0] = (_rope_wide(q, c, sp, sm) * (HEAD_DIM ** -0.5 * LOG2E)).astype(BF16)
    k = jnp.dot(h, w_ref[:, d:2 * d], preferred_element_type=F32)
    k_ref[0] = _rope_wide(k, c, sp, sm).astype(BF16)
    hw = 2 * HEAD_DIM
    v_t = jnp.dot(h, w_ref[:, 2 * d:3 * d], preferred_element_type=F32).T.astype(BF16)
    ones_rows = _ones_rows(v_t.shape[1])
    for hh in range(d // hw):
        vt_ref[0, hh, 0, :hw, :] = v_t[hh * hw:(hh + 1) * hw]
        vt_ref[0, hh, 0, hw:, :] = ones_rows


def _diff_proj(x, g, w_in, tables, tm):
    B, S, D = x.shape
    hw = 2 * HEAD_DIM
    H = D // hw
    c, sp, sm = tables
    tab_spec = pl.BlockSpec((tm, LANES), lambda b, i: (i, 0))
    tok_spec = pl.BlockSpec((1, tm, D), lambda b, i: (b, i, 0))
    return pl.pallas_call(
        functools.partial(_diff_proj_kernel, d=D),
        grid=(B, S // tm),
        in_specs=[tok_spec, pl.BlockSpec((1, D), lambda b, i: (0, 0)),
                  pl.BlockSpec((D, 3 * D), lambda b, i: (0, 0)), tab_spec, tab_spec, tab_spec],
        out_specs=[tok_spec, tok_spec,
                   pl.BlockSpec((1, H, 1, hw + ONES_ROWS, tm), lambda b, i: (b, 0, i, 0, 0))],
        out_shape=[jax.ShapeDtypeStruct((B, S, D), BF16), jax.ShapeDtypeStruct((B, S, D), BF16),
                   jax.ShapeDtypeStruct((B, H, S // tm, hw + ONES_ROWS, tm), BF16)],
        compiler_params=_params(("parallel", "parallel")),
        name="diff_proj",
    )(x, g.reshape(1, D), w_in.astype(BF16), c, sp, sm)


def _diff_attn_kernel(lq1_ref, lk1_ref, lq2_ref, lk2_ref, sg_ref, q_ref, k_ref, vt_ref, causal_ref,
                      o_ref, qat_ref, *scratch, tq, tk, heads, lambda_init):
    hw = 2 * HEAD_DIM
    nq = q_ref.shape[1] // tq
    ia = pl.program_id(2)
    ib = nq - 1 - ia
    pair = ((0, ia), (1, ib))
    rows = lambda i: pl.ds(pl.multiple_of(i * tq, tq), tq)
    lam = (jnp.exp(jnp.sum(lq1_ref[...] * lk1_ref[...], axis=1, keepdims=True))
           - jnp.exp(jnp.sum(lq2_ref[...] * lk2_ref[...], axis=1, keepdims=True)) + lambda_init)
    for c in range(heads):
        for x, i in pair:
            qt = q_ref[0, rows(i), c * hw:(c + 1) * hw].astype(F32).T
            row = lax.broadcasted_iota(jnp.int32, qt.shape, 0)
            qat_ref[c, x] = jnp.concatenate([jnp.where(row < HEAD_DIM, qt, 0.0),
                                             jnp.where(row >= HEAD_DIM, qt, 0.0)], axis=1).astype(BF16)
    _flash_pair(
        qat_ref,
        lambda c, j: k_ref[0, pl.ds(pl.multiple_of(j * tk, tk), tk), c * hw:(c + 1) * hw],
        lambda c, j: vt_ref[0, c, j],
        causal_ref, ia, ib, tq, tk, (nq * tq) // tk + 1, scratch)
    acc_ref = scratch[-1]
    for c in range(heads):
        for x, i in pair:
            acc = acc_ref[c, x]
            o = acc[:hw] / acc[hw:hw + 1]
            o = o[:, :tq] - lam * o[:, tq:]
            o = o * lax.rsqrt(jnp.mean(o * o, axis=0, keepdims=True) + EPS) * sg_ref[...]
            o_ref[0, rows(i), c * hw:(c + 1) * hw] = (o * (1.0 - lambda_init)).T.astype(BF16)


def _diff_attn(q, k, vt, lq1, lk1, lq2, lk2, subln_g, lambda_init, tq, tk, heads):
    B, S, D = q.shape
    hw = 2 * HEAD_DIM
    H = D // hw
    nt = S // tk
    vrows = hw + ONES_ROWS
    causal = _causal_table(tq, tk)
    vec = lambda a: a.reshape(1, -1).astype(F32)
    vec_spec = lambda n: pl.BlockSpec((1, n), lambda b, h, i: (0, 0))
    seq_spec = pl.BlockSpec((1, S, heads * hw), lambda b, h, i: (b, 0, h))
    return pl.pallas_call(
        functools.partial(_diff_attn_kernel, tq=tq, tk=tk, heads=heads, lambda_init=lambda_init),
        grid=(B, H // heads, S // tq // 2),
        in_specs=[vec_spec(HEAD_DIM)] * 4 + [
            pl.BlockSpec((hw, 1), lambda b, h, i: (0, 0)),
            seq_spec,
            seq_spec,
            pl.BlockSpec((1, heads, nt, vrows, tk), lambda b, h, i: (b, h, 0, 0, 0)),
            pl.BlockSpec(causal.shape, lambda b, h, i: (0, 0)),
        ],
        out_specs=seq_spec,
        out_shape=jax.ShapeDtypeStruct((B, S, D), BF16),
        scratch_shapes=[pltpu.VMEM((heads, 2, hw, 2 * tq), BF16)]
        + _flash_pair_scratch(heads, tk, 2 * tq, vrows),
        compiler_params=_params(("parallel", "parallel", "arbitrary")),
        name="diff_attn",
    )(vec(lq1), vec(lk1), vec(lq2), vec(lk2), subln_g.reshape(hw, 1).astype(F32), q, k, vt, causal)


def _post_kernel(x_ref, a_ref, wo_ref, g_ref, wu_ref, wd_ref, fg_ref, o_ref, *, ff_chunk, final):
    x = x_ref[...] + jnp.dot(a_ref[...], wo_ref[...], preferred_element_type=F32)
    h = _rms(x, g_ref[...]).astype(BF16)
    y = x
    for lo in range(0, wu_ref.shape[1], ff_chunk):
        u = jnp.maximum(jnp.dot(h, wu_ref[:, lo:lo + ff_chunk], preferred_element_type=F32), 0.0)
        y = y + jnp.dot((u * u).astype(BF16), wd_ref[lo:lo + ff_chunk, :], preferred_element_type=F32)
    if final:
        y = _rms(y, fg_ref[...])
    o_ref[...] = y


def _post(x, a, w_out, g, w_up, w_down, final_g, tm, ff_chunk):
    B, S, D = x.shape
    T = B * S
    dff = w_up.shape[1]
    final = final_g is not None
    fg = (final_g if final else jnp.ones((D,), F32)).reshape(1, D)
    const = lambda shape: pl.BlockSpec(shape, lambda i: (0, 0), pipeline_mode=pl.Buffered(1))
    tok = pl.BlockSpec((tm, D), lambda i: (i, 0))
    out = pl.pallas_call(
        functools.partial(_post_kernel, ff_chunk=ff_chunk, final=final),
        grid=(T // tm,),
        in_specs=[tok, tok, const((D, D)), const((1, D)), const((D, dff)), const((dff, D)),
                  const((1, D))],
        out_specs=tok,
        out_shape=jax.ShapeDtypeStruct((T, D), F32),
        compiler_params=_params(("parallel",)),
        name="post_mlp",
    )(x.reshape(T, D), a.reshape(T, D), w_out.astype(BF16), g.reshape(1, D),
      w_up.astype(BF16), w_down.astype(BF16), fg)
    return out.reshape(B, S, D)


def kernel(x, attn_norm_g, mlp_norm_g, nsa_w_in, nsa_ck_pos, nsa_ck_w1, nsa_ck_w2, nsa_cv_pos, nsa_cv_w1, nsa_cv_w2, nsa_w_out, diff_w_in, diff_lq1, diff_lk1, diff_lq2, diff_lk2, diff_subln_g, diff_w_out, mlp_w_up, mlp_w_down, final_norm_g):
    B, S, D = x.shape
    depth = attn_norm_g.shape[0]
    tables = _rope_tables(jnp.arange(S))
    tm = min(512, S)
    tk = tm
    for i in range(depth):
        j = i // 2
        if i % 2 == 0:
            q, kvc, ks, vst, kw, vwt, gates = _nsa_proj(x, attn_norm_g[i], nsa_w_in[j], tables, tm)
            kvc_c, vct = _compress(kvc,
                                   jnp.stack([nsa_ck_pos[j], nsa_cv_pos[j]]),
                                   jnp.stack([nsa_ck_w1[j], nsa_cv_w1[j]]),
                                   jnp.stack([nsa_ck_w2[j], nsa_cv_w2[j]]))
            a = _nsa_attn(q, gates, kvc_c, vct, ks, vst, kw, vwt, tq=256, tk=tk)
            w_out = nsa_w_out[j]
        else:
            lambda_init = 0.8 - 0.6 * math.exp(-0.3 * i)
            q, k, vt = _diff_proj(x, attn_norm_g[i], diff_w_in[j], tables, tm)
            a = _diff_attn(q, k, vt, diff_lq1[j], diff_lk1[j], diff_lq2[j], diff_lk2[j],
                           diff_subln_g[j], lambda_init, tq=tk, tk=tk, heads=2)
            w_out = diff_w_out[j]
        x = _post(x, a, w_out, mlp_norm_g[i], mlp_w_up[i], mlp_w_down[i],
                  final_norm_g if i == depth - 1 else None, tm, ff_chunk=1024)
    return x
```

```python
import functools
import math

import jax
import jax.numpy as jnp
from jax import lax
from jax.experimental import pallas as pl
from jax.experimental.pallas import tpu as pltpu

F32 = jnp.float32
BF16 = jnp.bfloat16

EPS = 1e-6
ROPE_THETA = 500000.0
HEAD_DIM = 64
ROT_HALF = HEAD_DIM // 8
LANES = 128
NEG_INF = -1e30
BIG = 1e9
LOG2E = math.log2(math.e)
ONES_ROWS = 16

NSA_GROUPS = 4
NSA_HPG = 4
CMP_BLOCK = 32
CMP_STRIDE = 16
SLC_BLOCK = 64
SLC_TOPK = 16
WINDOW = 512
WIN_TILE = 128

VMEM_LIMIT = 56 * 1024 * 1024


def _params(sem):
    return pltpu.CompilerParams(dimension_semantics=sem, vmem_limit_bytes=VMEM_LIMIT)


def _rms(x, g):
    return x * lax.rsqrt(jnp.mean(x * x, axis=-1, keepdims=True) + EPS) * g


def _rope_tables(positions):
    n = positions.shape[0]
    rot = 2 * ROT_HALF
    inv = 1.0 / (ROPE_THETA ** (jnp.arange(0, rot, 2, dtype=F32) / rot))
    ang = positions.astype(F32)[:, None] * inv[None, :]
    cos, sin = jnp.cos(ang), jnp.sin(ang)
    rest = HEAD_DIM - rot
    c = jnp.concatenate([cos, cos, jnp.ones((n, rest), F32)], axis=1)
    sp = jnp.concatenate([jnp.zeros((n, ROT_HALF), F32), sin, jnp.zeros((n, rest), F32)], axis=1)
    sm = jnp.concatenate([-sin, jnp.zeros((n, HEAD_DIM - ROT_HALF), F32)], axis=1)
    reps = LANES // HEAD_DIM
    return tuple(jnp.tile(t, (1, reps)) for t in (c, sp, sm))


def _rope(y, c, sp, sm):
    return (y * c + pltpu.roll(y, ROT_HALF, 1) * sp
            + pltpu.roll(y, LANES - ROT_HALF, 1) * sm)


def _rope_wide(y, c, sp, sm):
    n = y.shape[1] // LANES
    return jnp.concatenate(
        [_rope(y[:, k * LANES:(k + 1) * LANES], c, sp, sm) for k in range(n)], axis=1)


def _ones_rows(width):
    row = lax.broadcasted_iota(jnp.int32, (ONES_ROWS, width), 0)
    return jnp.where(row == 0, 1.0, 0.0).astype(BF16)


def _nsa_proj_kernel(x_ref, g_ref, w_ref, c_ref, sp_ref, sm_ref,
                     q_ref, kvc_ref, ks_ref, vs_ref, kw_ref, vw_ref, gate_ref, *stage_refs, tm, d):
    i = pl.program_id(1)
    h = _rms(x_ref[0], g_ref[...]).astype(BF16)
    c, sp, sm = c_ref[...], sp_ref[...], sm_ref[...]
    kvw = NSA_GROUPS * HEAD_DIM

    def proj(lo, width):
        return jnp.dot(h, w_ref[:, lo:lo + width], preferred_element_type=F32)

    q = proj(0, d)
    q_ref[0] = (_rope_wide(q, c, sp, sm) * (HEAD_DIM ** -0.5 * LOG2E)).astype(BF16)
    kvc = proj(d, 2 * kvw)
    per_chunk = LANES // HEAD_DIM
    for j, stage_ref in enumerate(stage_refs):
        stage_ref[...] = kvc[:, j * LANES:(j + 1) * LANES]
        taps = [stage_ref[pl.ds(l, tm // CMP_STRIDE, stride=CMP_STRIDE), :].astype(BF16)
                for l in range(CMP_STRIDE)]
        for u in range(per_chunk):
            kg = j * per_chunk + u
            sl = slice(u * HEAD_DIM, (u + 1) * HEAD_DIM)
            kvc_ref[kg // NSA_GROUPS, 0, kg % NSA_GROUPS] = jnp.concatenate(
                [t[:, sl] for t in taps], axis=1)

    ksl = _rope_wide(proj(d + 2 * kvw, kvw), c, sp, sm)
    vsl = proj(d + 3 * kvw, kvw)
    kwn = _rope_wide(proj(d + 4 * kvw, kvw), c, sp, sm)
    vwn = proj(d + 5 * kvw, kvw)
    nblk = ks_ref.shape[3] - HEAD_DIM
    blk = (lax.broadcasted_iota(jnp.int32, (tm, nblk), 0) // SLC_BLOCK) + i * (tm // SLC_BLOCK)
    onehot = jnp.where(blk == lax.broadcasted_iota(jnp.int32, (tm, nblk), 1), 1.0, 0.0)
    vsl_t, vwn_t = vsl.T.astype(BF16), vwn.T.astype(BF16)
    ones_rows = _ones_rows(tm)
    for g in range(NSA_GROUPS):
        sl = slice(g * HEAD_DIM, (g + 1) * HEAD_DIM)
        ks_ref[0, g] = jnp.concatenate([ksl[:, sl], onehot], axis=1).astype(BF16)
        kw_ref[0, g] = kwn[:, sl].astype(BF16)
        vs_ref[0, g, 0, :HEAD_DIM, :] = vsl_t[sl]
        vs_ref[0, g, 0, HEAD_DIM:, :] = ones_rows
        for x in range(tm // WIN_TILE):
            vw_ref[0, g, x, :HEAD_DIM, :] = vwn_t[sl, x * WIN_TILE:(x + 1) * WIN_TILE]
            vw_ref[0, g, x, HEAD_DIM:, :] = ones_rows[:, :WIN_TILE]
    gate_ref[0] = jax.nn.sigmoid(proj(d + 6 * kvw, NSA_GROUPS * LANES))


def _nsa_proj(x, g, w_in, tables, tm):
    B, S, D = x.shape
    G, hd = NSA_GROUPS, HEAD_DIM
    kvw = G * hd
    nblk = S // SLC_BLOCK
    n_main = D + 6 * kvw
    wg = w_in[:, n_main:].reshape(D, G, NSA_HPG * 3)
    wg = jnp.pad(wg, ((0, 0), (0, 0), (0, LANES - NSA_HPG * 3))).reshape(D, G * LANES)
    w = jnp.concatenate([w_in[:, :n_main], wg], axis=1).astype(BF16)
    nw = w.shape[1]
    c, sp, sm = tables
    vrows = hd + ONES_ROWS
    wtiles = tm // WIN_TILE
    tab_spec = pl.BlockSpec((tm, LANES), lambda b, i: (i, 0))
    kv_spec = lambda width: pl.BlockSpec((1, G, tm, width), lambda b, i: (b, 0, i, 0))
    return pl.pallas_call(
        functools.partial(_nsa_proj_kernel, tm=tm, d=D),
        grid=(B, S // tm),
        in_specs=[
            pl.BlockSpec((1, tm, D), lambda b, i: (b, i, 0)),
            pl.BlockSpec((1, D), lambda b, i: (0, 0)),
            pl.BlockSpec((D, nw), lambda b, i: (0, 0)),
            tab_spec, tab_spec, tab_spec,
        ],
        out_specs=[
            pl.BlockSpec((1, tm, D), lambda b, i: (b, i, 0)),
            pl.BlockSpec((2, 1, G, tm // CMP_STRIDE, CMP_STRIDE * hd), lambda b, i: (0, b, 0, i, 0)),
            kv_spec(hd + nblk),
            pl.BlockSpec((1, G, 1, vrows, tm), lambda b, i: (b, 0, i, 0, 0)),
            kv_spec(hd),
            pl.BlockSpec((1, G, wtiles, vrows, WIN_TILE), lambda b, i: (b, 0, i, 0, 0)),
            pl.BlockSpec((1, tm, G * LANES), lambda b, i: (b, i, 0)),
        ],
        out_shape=[
            jax.ShapeDtypeStruct((B, S, D), BF16),
            jax.ShapeDtypeStruct((2, B, G, S // CMP_STRIDE, CMP_STRIDE * hd), BF16),
            jax.ShapeDtypeStruct((B, G, S, hd + nblk), BF16),
            jax.ShapeDtypeStruct((B, G, S // tm, vrows, tm), BF16),
            jax.ShapeDtypeStruct((B, G, S, hd), BF16),
            jax.ShapeDtypeStruct((B, G, S // WIN_TILE, vrows, WIN_TILE), BF16),
            jax.ShapeDtypeStruct((B, S, G * LANES), F32),
        ],
        scratch_shapes=[pltpu.VMEM((tm, LANES), F32)] * (2 * kvw // LANES),
        compiler_params=_params(("parallel", "parallel")),
        name="nsa_proj",
    )(x, g.reshape(1, D), w, c, sp, sm)


def _compress_kernel(r_ref, pa_ref, pb_ref, w1a_ref, w1b_ref, w2_ref, c_ref, sp_ref, sm_ref,
                     o_ref, ot_ref):
    r = r_ref[0, 0, 0]
    rows = r.shape[0]
    a = jnp.dot(r, w1a_ref[0], preferred_element_type=F32)
    b = jnp.dot(r, w1b_ref[0], preferred_element_type=F32)
    bias = (jnp.dot(pa_ref[0], w1a_ref[0], preferred_element_type=F32)
            + jnp.dot(pb_ref[0], w1b_ref[0], preferred_element_type=F32))
    hid = a + pltpu.roll(b, rows - 1, 0) + bias[0:1]
    hid = jax.nn.gelu(hid).astype(BF16)
    y = jnp.dot(hid, w2_ref[0], preferred_element_type=F32)
    y = _rope(y, c_ref[0], sp_ref[0], sm_ref[0])
    o_ref[0, 0, 0] = y[:, :HEAD_DIM].astype(BF16)
    ot_ref[0, 0, 0] = y.T[:HEAD_DIM].astype(BF16)


def _compress(r, pos, w1, w2):
    _, B, G, rows, rw = r.shape
    hd = HEAD_DIM
    w1 = w1.astype(BF16)
    hidden = w1.shape[2]
    w1a, w1b = w1[:, :rw], w1[:, rw:]
    posr = pos.reshape(2, 2, 1, rw).astype(BF16)
    pa = jnp.broadcast_to(posr[:, 0], (2, 16, rw))
    pb = jnp.broadcast_to(posr[:, 1], (2, 16, rw))
    w2p = jnp.pad(w2, ((0, 0), (0, 0), (0, LANES - hd))).astype(BF16)
    cmp_end = jnp.arange(rows) * CMP_STRIDE + CMP_BLOCK - 1
    c, sp, sm = _rope_tables(cmp_end)
    ident = (jnp.ones_like(c), jnp.zeros_like(sp), jnp.zeros_like(sm))
    c, sp, sm = (jnp.stack([t, u]) for t, u in zip((c, sp, sm), ident))
    per_kind = lambda shape: pl.BlockSpec((1,) + shape, lambda k, b, g: (k,) + (0,) * len(shape))
    return pl.pallas_call(
        _compress_kernel,
        grid=(2, B, G),
        in_specs=[
            pl.BlockSpec((1, 1, 1, rows, rw), lambda k, b, g: (k, b, g, 0, 0)),
            per_kind((16, rw)), per_kind((16, rw)),
            per_kind((rw, hidden)), per_kind((rw, hidden)), per_kind((hidden, LANES)),
            per_kind((rows, LANES)), per_kind((rows, LANES)), per_kind((rows, LANES)),
        ],
        out_specs=[pl.BlockSpec((1, 1, 1, rows, hd), lambda k, b, g: (k, b, g, 0, 0)),
                   pl.BlockSpec((1, 1, 1, hd, rows), lambda k, b, g: (k, b, g, 0, 0))],
        out_shape=[jax.ShapeDtypeStruct((2, B, G, rows, hd), BF16),
                   jax.ShapeDtypeStruct((2, B, G, hd, rows), BF16)],
        compiler_params=_params(("parallel", "parallel", "parallel")),
        name="nsa_compress",
    )(r, pa, pb, w1a, w1b, w2p, c, sp, sm)


def _causal_table(tq, tk):
    y = jnp.arange(2 * tk - tq)[:, None] - (tk - tq)
    return jnp.where(y <= jnp.arange(tq)[None, :], 0.0, NEG_INF).astype(F32)


def _tile_lanes(bias, s):
    tq = bias.shape[1]
    return jnp.concatenate(
        [s[:, x:x + tq] + bias for x in range(0, s.shape[1], tq)], axis=1)


def _flash_pair_scratch(chains, tk, r, vrows):
    return [pltpu.VMEM((chains, 2, tk, r), F32), pltpu.VMEM((chains, 2, tk, r), BF16),
            pltpu.VMEM((chains, 2, 1, r), F32), pltpu.VMEM((chains, 2, 1, r), F32),
            pltpu.VMEM((chains, 2, vrows, r), F32)]


def _flash_pair(qat_ref, k_at, vt_at, causal_ref, ia, ib, tq, tk, nsteps, scratch):
    s_ref, p_ref, al_ref, m_ref, acc_ref = scratch
    chains = range(qat_ref.shape[0])
    nfull = [(i * tq) // tk for i in (ia, ib)]

    def info(t):
        if isinstance(t, int) and t < 2:
            return t, nfull[t], True
        x = jnp.where(t - 2 >= nfull[0], 1, 0)
        return x, t - 2 - x * nfull[0], False

    def scores(t, slot):
        x, tile, diag = info(t)
        for c in chains:
            s = jnp.dot(k_at(c, tile), qat_ref[c, x], preferred_element_type=F32)
            if diag:
                i = (ia, ib)[x]
                start = pl.multiple_of((tk - tq) - (i * tq - nfull[x] * tk), tq)
                s = _tile_lanes(causal_ref[pl.ds(start, tk), :], s)
            s_ref[c, slot] = s

    def probs(t, slot):
        x, _, diag = info(t)
        for c in chains:
            m_new = jnp.max(s_ref[c, slot], axis=0, keepdims=True)
            if not diag:
                m = m_ref[c, x]
                m_new = jnp.maximum(m, m_new)
                al_ref[c, slot] = jnp.exp2(m - m_new)
            m_ref[c, x] = m_new
            p_ref[c, slot] = jnp.exp2(s_ref[c, slot] - m_new).astype(BF16)

    def accum(t, slot):
        x, tile, diag = info(t)
        for c in chains:
            pv = jnp.dot(vt_at(c, tile), p_ref[c, slot], preferred_element_type=F32)
            acc_ref[c, x] = pv if diag else al_ref[c, slot] * acc_ref[c, x] + pv

    def stage(t, a):
        scores(t + 2, a)
        probs(t + 1, 1 - a)
        accum(t, a)

    scores(0, 0)
    probs(0, 0)
    scores(1, 1)
    stage(0, 0)
    stage(1, 1)
    steady = nsteps - 4
    pairs = steady // 2

    def body(k, carry):
        stage(2 * k + 2, 0)
        stage(2 * k + 3, 1)
        return carry

    lax.fori_loop(0, pairs, body, 0)
    a = 0
    if steady % 2:
        stage(nsteps - 3, 0)
        a = 1
    accum(nsteps - 2, a)
    probs(nsteps - 1, 1 - a)
    accum(nsteps - 1, 1 - a)


def _block_penalty(imp, t0, tq, topk):
    ns = imp.shape[0]
    jblk = lax.broadcasted_iota(jnp.int32, (ns, tq), 0)
    tcol = t0 + lax.broadcasted_iota(jnp.int32, (ns, tq), 1)
    cur = tcol // SLC_BLOCK
    forced = (jblk == 0) | (jblk == cur) | (jblk == cur - 1)
    future = jblk * SLC_BLOCK > tcol
    val = jnp.where(forced, BIG, jnp.where(future, -BIG, imp))
    sub = 8
    vals = [val[a:a + sub] for a in range(0, ns, sub)]
    ranks = [jnp.zeros((sub, tq), F32) for _ in vals]
    row = lax.broadcasted_iota(jnp.int32, (sub, tq), 0)
    for i in range(ns):
        vi = jnp.broadcast_to(val[i:i + 1, :], (sub, tq))
        for a, va in enumerate(vals):
            if a * sub > i:
                hit = jnp.where(vi >= va, 1.0, 0.0)
            elif a * sub + sub - 1 < i:
                hit = jnp.where(vi > va, 1.0, 0.0)
            else:
                hit = jnp.where(row > i - a * sub, jnp.where(vi >= va, 1.0, 0.0),
                                jnp.where(vi > va, 1.0, 0.0))
            ranks[a] = ranks[a] + hit
    return jnp.concatenate([jnp.where(r < topk, 0.0, NEG_INF) for r in ranks], axis=0)


def _nsa_attn_kernel(qa_ref, qb_ref, ga_ref, gb_ref, kc_ref, vct_ref, selt_ref, ks_ref, vst_ref, kw_ref,
                     vwt_ref, cmask_ref, wmask_ref, causal_ref, o_ref, qat_ref, pre_ref, *scratch,
                     tq, tk, groups, topk):
    P, hd = NSA_HPG, HEAD_DIM
    R = P * tq
    qw = P * hd
    nq = o_ref.shape[1] // tq
    ia = pl.program_id(2)
    pair = ((0, ia, qa_ref, ga_ref), (1, nq - 1 - ia, qb_ref, gb_ref))
    selt = selt_ref[...]
    ns, ncp = selt.shape
    wk = WINDOW + tq
    gate_t = lambda g_ref, c: g_ref[0, :, c * LANES:(c + 1) * LANES].T
    cols = lambda p: slice(p * tq, (p + 1) * tq)

    for x, i, q_ref, g_ref in pair:
        t0 = i * tq
        wstart = pl.multiple_of(jnp.maximum(t0 - WINDOW, 0), tq)
        nck, nsb = (ncp // 2, ns // 2) if x == 0 else (ncp, ns)
        for c in range(groups):
            qt = q_ref[0, :, c * qw:(c + 1) * qw].astype(F32).T
            q_t = jnp.concatenate([qt[p * hd:(p + 1) * hd] for p in range(P)], axis=1).astype(BF16)
            gt = gate_t(g_ref, c)

            s = jnp.dot(kc_ref[0, 0, c, 0:nck, :], q_t, preferred_element_type=F32)
            s = _tile_lanes(cmask_ref[pl.ds(pl.multiple_of(ncp - t0 // CMP_STRIDE, 8), nck), :], s)
            m_c = jnp.max(s, axis=0, keepdims=True)
            p_c = jnp.exp2(s - m_c)
            l_c = jnp.sum(p_c, axis=0, keepdims=True)
            p_c = p_c * jnp.where(m_c > 0.5 * NEG_INF, 1.0 / l_c, 0.0)
            o_c = jnp.dot(vct_ref[0, 0, c, :, 0:nck], p_c.astype(BF16),
                          preferred_element_type=F32)

            rem = p_c[:, cols(0)]
            for p in range(1, P):
                rem = rem + p_c[:, cols(p)]
            imp = jnp.zeros((nsb, tq), F32)
            for _ in range(3):
                part = rem.astype(BF16)
                imp = imp + jnp.dot(selt[0:nsb, 0:nck], part, preferred_element_type=F32)
                rem = rem - part.astype(F32)
            pen = _block_penalty(imp, t0, tq, topk).astype(BF16)
            if nsb < ns:
                pen = jnp.concatenate([pen, jnp.zeros((ns - nsb, tq), BF16)], axis=0)
            qat_ref[c, x] = jnp.concatenate([q_t, jnp.concatenate([pen] * P, axis=1)], axis=0)

            s = jnp.dot(kw_ref[0, c, pl.ds(wstart, wk), :], q_t, preferred_element_type=F32)
            s = _tile_lanes(wmask_ref[pl.ds(pl.multiple_of(WINDOW - (t0 - wstart), tq), wk), :], s)
            p_w = jnp.exp2(s - jnp.max(s, axis=0, keepdims=True)).astype(BF16)
            wt0 = wstart // WIN_TILE
            acc_w = jnp.zeros((hd + ONES_ROWS, R), F32)
            for y in range(wk // WIN_TILE):
                acc_w = acc_w + jnp.dot(vwt_ref[0, c, wt0 + y], p_w[y * WIN_TILE:(y + 1) * WIN_TILE],
                                        preferred_element_type=F32)
            o_w = acc_w[:hd] / acc_w[hd:hd + 1]

            for p in range(P):
                pre_ref[c, x, p * hd:(p + 1) * hd, :] = (gt[3 * p:3 * p + 1] * o_c[:, cols(p)]
                                                         + gt[3 * p + 2:3 * p + 3] * o_w[:, cols(p)])

    _flash_pair(
        qat_ref,
        lambda c, j: ks_ref[0, c, pl.ds(pl.multiple_of(j * tk, tk), tk), :],
        lambda c, j: vst_ref[0, c, j],
        causal_ref, ia, nq - 1 - ia, tq, tk, (nq * tq) // tk + 1, scratch)

    acc_ref = scratch[-1]
    for x, i, _, g_ref in pair:
        for c in range(groups):
            acc = acc_ref[c, x]
            o_s = acc[:hd] / acc[hd:hd + 1]
            gt = gate_t(g_ref, c)
            out_t = jnp.concatenate(
                [pre_ref[c, x, p * hd:(p + 1) * hd, :] + gt[3 * p + 1:3 * p + 2] * o_s[:, cols(p)]
                 for p in range(P)], axis=0)
            o_ref[0, pl.ds(pl.multiple_of(i * tq, tq), tq), c * qw:(c + 1) * qw] = out_t.T.astype(BF16)


def _nsa_attn(q, gates, kvc_c, vct, ks, vst, kw, vwt, tq, tk):
    B, S, D = q.shape
    G, P, hd = NSA_GROUPS, NSA_HPG, HEAD_DIM
    groups = 2
    ncp = kvc_c.shape[3]
    ns = S // SLC_BLOCK
    nt = S // tk
    nwt = S // WIN_TILE
    topk = min(SLC_TOPK, ns)
    wk = WINDOW + tq
    tl = jnp.arange(tq)[None, :]
    y = jnp.arange(2 * ncp)[:, None] - ncp
    cmask = jnp.where(y * CMP_STRIDE + CMP_BLOCK - 1 <= tl, 0.0, NEG_INF).astype(F32)
    y = jnp.arange(WINDOW + wk)[:, None]
    wmask = jnp.where((y - WINDOW <= tl) & (y > tl), 0.0, NEG_INF).astype(F32)
    causal = _causal_table(tq, tk)
    whole = lambda a: pl.BlockSpec(a.shape, lambda b, g, i: (0,) * a.ndim)
    n = jnp.arange(ncp)
    c_start, c_end = n * CMP_STRIDE, n * CMP_STRIDE + CMP_BLOCK - 1
    sj = jnp.arange(ns) * SLC_BLOCK
    selt = ((c_end[None, :] >= sj[:, None]) & (c_start[None, :] <= sj[:, None] + SLC_BLOCK - 1)
            & (n[None, :] < ncp - 1)).astype(BF16)
    vrows = hd + ONES_ROWS
    nq = S // tq
    tile_a = lambda width: pl.BlockSpec((1, tq, groups * width), lambda b, g, i: (b, i, g))
    tile_b = lambda width: pl.BlockSpec((1, tq, groups * width), lambda b, g, i: (b, nq - 1 - i, g))
    return pl.pallas_call(
        functools.partial(_nsa_attn_kernel, tq=tq, tk=tk, groups=groups, topk=topk),
        grid=(B, G // groups, nq // 2),
        in_specs=[
            tile_a(P * hd), tile_b(P * hd),
            tile_a(LANES), tile_b(LANES),
            pl.BlockSpec((1, 1, groups, ncp, hd), lambda b, g, i: (0, b, g, 0, 0)),
            pl.BlockSpec((1, 1, groups, hd, ncp), lambda b, g, i: (1, b, g, 0, 0)),
            whole(selt),
            pl.BlockSpec((1, groups, S, hd + ns), lambda b, g, i: (b, g, 0, 0)),
            pl.BlockSpec((1, groups, nt, vrows, tk), lambda b, g, i: (b, g, 0, 0, 0)),
            pl.BlockSpec((1, groups, S, hd), lambda b, g, i: (b, g, 0, 0)),
            pl.BlockSpec((1, groups, nwt, vrows, WIN_TILE), lambda b, g, i: (b, g, 0, 0, 0)),
            whole(cmask), whole(wmask), whole(causal),
        ],
        out_specs=pl.BlockSpec((1, S, groups * P * hd), lambda b, g, i: (b, 0, g)),
        out_shape=jax.ShapeDtypeStruct((B, S, D), BF16),
        scratch_shapes=[pltpu.VMEM((groups, 2, hd + ns, P * tq), BF16),
                        pltpu.VMEM((groups, 2, P * hd, tq), F32)]
        + _flash_pair_scratch(groups, tk, P * tq, vrows),
        compiler_params=_params(("parallel", "parallel", "arbitrary")),
        name="nsa_attn",
    )(q, q, gates, gates, kvc_c, vct, selt, ks, vst, kw, vwt, cmask, wmask, causal)


def _diff_proj_kernel(x_ref, g_ref, w_ref, c_ref, sp_ref, sm_ref, q_ref, k_ref, vt_ref, *, d):
    h = _rms(x_ref[0], g_ref[...]).astype(BF16)
    c, sp, sm = c_ref[...], sp_ref[...], sm_ref[...]
    q = jnp.dot(h, w_ref[:, 0:d], preferred_element_type=F32)
    q_ref[0] = (_rope_wide(q, c, sp, sm) * (HEAD_DIM ** -0.5 * LOG2E)).astype(BF16)
    k = jnp.dot(h, w_ref[:, d:2 * d], preferred_element_type=F32)
    k_ref[0] = _rope_wide(k, c, sp, sm).astype(BF16)
    hw = 2 * HEAD_DIM
    v_t = jnp.dot(h, w_ref[:, 2 * d:3 * d], preferred_element_type=F32).T.astype(BF16)
    ones_rows = _ones_rows(v_t.shape[1])
    for hh in range(d // hw):
        vt_ref[0, hh, 0, :hw, :] = v_t[hh * hw:(hh + 1) * hw]
        vt_ref[0, hh, 0, hw:, :] = ones_rows


def _diff_proj(x, g, w_in, tables, tm):
    B, S, D = x.shape
    hw = 2 * HEAD_DIM
    H = D // hw
    c, sp, sm = tables
    tab_spec = pl.BlockSpec((tm, LANES), lambda b, i: (i, 0))
    tok_spec = pl.BlockSpec((1, tm, D), lambda b, i: (b, i, 0))
    return pl.pallas_call(
        functools.partial(_diff_proj_kernel, d=D),
        grid=(B, S // tm),
        in_specs=[tok_spec, pl.BlockSpec((1, D), lambda b, i: (0, 0)),
                  pl.BlockSpec((D, 3 * D), lambda b, i: (0, 0)), tab_spec, tab_spec, tab_spec],
        out_specs=[tok_spec, tok_spec,
                   pl.BlockSpec((1, H, 1, hw + ONES_ROWS, tm), lambda b, i: (b, 0, i, 0, 0))],
        out_shape=[jax.ShapeDtypeStruct((B, S, D), BF16), jax.ShapeDtypeStruct((B, S, D), BF16),
                   jax.ShapeDtypeStruct((B, H, S // tm, hw + ONES_ROWS, tm), BF16)],
        compiler_params=_params(("parallel", "parallel")),
        name="diff_proj",
    )(x, g.reshape(1, D), w_in.astype(BF16), c, sp, sm)


def _diff_attn_kernel(lq1_ref, lk1_ref, lq2_ref, lk2_ref, sg_ref, q_ref, k_ref, vt_ref, causal_ref,
                      o_ref, qat_ref, *scratch, tq, tk, heads, lambda_init):
    hw = 2 * HEAD_DIM
    nq = q_ref.shape[1] // tq
    ia = pl.program_id(2)
    ib = nq - 1 - ia
    pair = ((0, ia), (1, ib))
    rows = lambda i: pl.ds(pl.multiple_of(i * tq, tq), tq)
    lam = (jnp.exp(jnp.sum(lq1_ref[...] * lk1_ref[...], axis=1, keepdims=True))
           - jnp.exp(jnp.sum(lq2_ref[...] * lk2_ref[...], axis=1, keepdims=True)) + lambda_init)
    for c in range(heads):
        for x, i in pair:
            qt = q_ref[0, rows(i), c * hw:(c + 1) * hw].astype(F32).T
            row = lax.broadcasted_iota(jnp.int32, qt.shape, 0)
            qat_ref[c, x] = jnp.concatenate([jnp.where(row < HEAD_DIM, qt, 0.0),
                                             jnp.where(row >= HEAD_DIM, qt, 0.0)], axis=1).astype(BF16)
    _flash_pair(
        qat_ref,
        lambda c, j: k_ref[0, pl.ds(pl.multiple_of(j * tk, tk), tk), c * hw:(c + 1) * hw],
        lambda c, j: vt_ref[0, c, j],
        causal_ref, ia, ib, tq, tk, (nq * tq) // tk + 1, scratch)
    acc_ref = scratch[-1]
    for c in range(heads):
        for x, i in pair:
            acc = acc_ref[c, x]
            o = acc[:hw] / acc[hw:hw + 1]
            o = o[:, :tq] - lam * o[:, tq:]
            o = o * lax.rsqrt(jnp.mean(o * o, axis=0, keepdims=True) + EPS) * sg_ref[...]
            o_ref[0, rows(i), c * hw:(c + 1) * hw] = (o * (1.0 - lambda_init)).T.astype(BF16)


def _diff_attn(q, k, vt, lq1, lk1, lq2, lk2, subln_g, lambda_init, tq, tk, heads):
    B, S, D = q.shape
    hw = 2 * HEAD_DIM
    H = D // hw
    nt = S // tk
    vrows = hw + ONES_ROWS
    causal = _causal_table(tq, tk)
    vec = lambda a: a.reshape(1, -1).astype(F32)
    vec_spec = lambda n: pl.BlockSpec((1, n), lambda b, h, i: (0, 0))
    seq_spec = pl.BlockSpec((1, S, heads * hw), lambda b, h, i: (b, 0, h))
    return pl.pallas_call(
        functools.partial(_diff_attn_kernel, tq=tq, tk=tk, heads=heads, lambda_init=lambda_init),
        grid=(B, H // heads, S // tq // 2),
        in_specs=[vec_spec(HEAD_DIM)] * 4 + [
            pl.BlockSpec((hw, 1), lambda b, h, i: (0, 0)),
            seq_spec,
            seq_spec,
            pl.BlockSpec((1, heads, nt, vrows, tk), lambda b, h, i: (b, h, 0, 0, 0)),
            pl.BlockSpec(causal.shape, lambda b, h, i: (0, 0)),
        ],
        out_specs=seq_spec,
        out_shape=jax.ShapeDtypeStruct((B, S, D), BF16),
        scratch_shapes=[pltpu.VMEM((heads, 2, hw, 2 * tq), BF16)]
        + _flash_pair_scratch(heads, tk, 2 * tq, vrows),
        compiler_params=_params(("parallel", "parallel", "arbitrary")),
        name="diff_attn",
    )(vec(lq1), vec(lk1), vec(lq2), vec(lk2), subln_g.reshape(hw, 1).astype(F32), q, k, vt, causal)


def _post_kernel(x_ref, a_ref, wo_ref, g_ref, wu_ref, wd_ref, fg_ref, o_ref, *, ff_chunk, final):
    x = x_ref[...] + jnp.dot(a_ref[...], wo_ref[...], preferred_element_type=F32)
    h = _rms(x, g_ref[...]).astype(BF16)
    y = x
    for lo in range(0, wu_ref.shape[1], ff_chunk):
        u = jnp.maximum(jnp.dot(h, wu_ref[:, lo:lo + ff_chunk], preferred_element_type=F32), 0.0)
        y = y + jnp.dot((u * u).astype(BF16), wd_ref[lo:lo + ff_chunk, :], preferred_element_type=F32)
    if final:
        y = _rms(y, fg_ref[...])
    o_ref[...] = y


def _post(x, a, w_out, g, w_up, w_down, final_g, tm, ff_chunk):
    B, S, D = x.shape
    T = B * S
    dff = w_up.shape[1]
    final = final_g is not None
    fg = (final_g if final else jnp.ones((D,), F32)).reshape(1, D)
    const = lambda shape: pl.BlockSpec(shape, lambda i: (0, 0), pipeline_mode=pl.Buffered(1))
    tok = pl.BlockSpec((tm, D), lambda i: (i, 0))
    out = pl.pallas_call(
        functools.partial(_post_kernel, ff_chunk=ff_chunk, final=final),
        grid=(T // tm,),
        in_specs=[tok, tok, const((D, D)), const((1, D)), const((D, dff)), const((dff, D)),
                  const((1, D))],
        out_specs=tok,
        out_shape=jax.ShapeDtypeStruct((T, D), F32),
        compiler_params=_params(("parallel",)),
        name="post_mlp",
    )(x.reshape(T, D), a.reshape(T, D), w_out.astype(BF16), g.reshape(1, D),
      w_up.astype(BF16), w_down.astype(BF16), fg)
    return out.reshape(B, S, D)


def kernel(x, attn_norm_g, mlp_norm_g, nsa_w_in, nsa_ck_pos, nsa_ck_w1, nsa_ck_w2, nsa_cv_pos, nsa_cv_w1, nsa_cv_w2, nsa_w_out, diff_w_in, diff_lq1, diff_lk1, diff_lq2, diff_lk2, diff_subln_g, diff_w_out, mlp_w_up, mlp_w_down, final_norm_g):
    B, S, D = x.shape
    depth = attn_norm_g.shape[0]
    tables = _rope_tables(jnp.arange(S))
    tm = min(512, S)
    tk = tm
    for i in range(depth):
        j = i // 2
        if i % 2 == 0:
            q, kvc, ks, vst, kw, vwt, gates = _nsa_proj(x, attn_norm_g[i], nsa_w_in[j], tables, tm)
            kvc_c, vct = _compress(kvc,
                                   jnp.stack([nsa_ck_pos[j], nsa_cv_pos[j]]),
                                   jnp.stack([nsa_ck_w1[j], nsa_cv_w1[j]]),
                                   jnp.stack([nsa_ck_w2[j], nsa_cv_w2[j]]))
            a = _nsa_attn(q, gates, kvc_c, vct, ks, vst, kw, vwt, tq=256, tk=tk)
            w_out = nsa_w_out[j]
        else:
            lambda_init = 0.8 - 0.6 * math.exp(-0.3 * i)
            q, k, vt = _diff_proj(x, attn_norm_g[i], diff_w_in[j], tables, tm)
            a = _diff_attn(q, k, vt, diff_lq1[j], diff_lk1[j], diff_lq2[j], diff_lk2[j],
                           diff_subln_g[j], lambda_init, tq=tk, tk=tk, heads=2)
            w_out = diff_w_out[j]
        x = _post(x, a, w_out, mlp_norm_g[i], mlp_w_up[i], mlp_w_down[i],
                  final_norm_g if i == depth - 1 else None, tm, ff_chunk=1024)
    return x
```

```python
import functools
import math

import jax
import jax.numpy as jnp
from jax import lax
from jax.experimental import pallas as pl
from jax.experimental.pallas import tpu as pltpu

F32 = jnp.float32
BF16 = jnp.bfloat16

EPS = 1e-6
ROPE_THETA = 500000.0
HEAD_DIM = 64
ROT_HALF = HEAD_DIM // 8
LANES = 128
SUBLANES = 8
BF16_ROWS = 2 * SUBLANES
NEG_INF = -1e30
BIG = 1e9
LOG2E = math.log2(math.e)
ONES_ROWS = BF16_ROWS

NSA_GROUPS = 4
NSA_HPG = 4
CMP_BLOCK = 32
CMP_STRIDE = 16
SLC_BLOCK = 64
SLC_TOPK = 16
WINDOW = 512
WIN_TILE = 128

VMEM_LIMIT = 56 * 1024 * 1024


def _params(sem):
    return pltpu.CompilerParams(dimension_semantics=sem, vmem_limit_bytes=VMEM_LIMIT)


def _rms(x, g):
    return x * lax.rsqrt(jnp.mean(x * x, axis=-1, keepdims=True) + EPS) * g


def _rope_tables(positions):
    n = positions.shape[0]
    rot = 2 * ROT_HALF
    inv = 1.0 / (ROPE_THETA ** (jnp.arange(0, rot, 2, dtype=F32) / rot))
    ang = positions.astype(F32)[:, None] * inv[None, :]
    cos, sin = jnp.cos(ang), jnp.sin(ang)
    rest = HEAD_DIM - rot
    c = jnp.concatenate([cos, cos, jnp.ones((n, rest), F32)], axis=1)
    sp = jnp.concatenate([jnp.zeros((n, ROT_HALF), F32), sin, jnp.zeros((n, rest), F32)], axis=1)
    sm = jnp.concatenate([-sin, jnp.zeros((n, HEAD_DIM - ROT_HALF), F32)], axis=1)
    reps = LANES // HEAD_DIM
    return tuple(jnp.tile(t, (1, reps)) for t in (c, sp, sm))


def _rope(y, c, sp, sm):
    return (y * c + pltpu.roll(y, ROT_HALF, 1) * sp
            + pltpu.roll(y, LANES - ROT_HALF, 1) * sm)


def _rope_wide(y, c, sp, sm):
    n = y.shape[1] // LANES
    return jnp.concatenate(
        [_rope(y[:, k * LANES:(k + 1) * LANES], c, sp, sm) for k in range(n)], axis=1)


def _ones_rows(width):
    row = lax.broadcasted_iota(jnp.int32, (ONES_ROWS, width), 0)
    return jnp.where(row == 0, 1.0, 0.0).astype(BF16)


def _nsa_proj_kernel(x_ref, g_ref, w_ref, c_ref, sp_ref, sm_ref,
                     q_ref, kvc_ref, ks_ref, vs_ref, kw_ref, vw_ref, gate_ref, *stage_refs, tm, d):
    i = pl.program_id(1)
    h = _rms(x_ref[0], g_ref[...]).astype(BF16)
    c, sp, sm = c_ref[...], sp_ref[...], sm_ref[...]
    kvw = NSA_GROUPS * HEAD_DIM

    def proj(lo, width):
        return jnp.dot(h, w_ref[:, lo:lo + width], preferred_element_type=F32)

    q = proj(0, d)
    q_ref[0] = (_rope_wide(q, c, sp, sm) * (HEAD_DIM ** -0.5 * LOG2E)).astype(BF16)
    kvc = proj(d, 2 * kvw)
    per_chunk = LANES // HEAD_DIM
    for j, stage_ref in enumerate(stage_refs):
        stage_ref[...] = kvc[:, j * LANES:(j + 1) * LANES]
        taps = [stage_ref[pl.ds(l, tm // CMP_STRIDE, stride=CMP_STRIDE), :].astype(BF16)
                for l in range(CMP_STRIDE)]
        for u in range(per_chunk):
            kg = j * per_chunk + u
            sl = slice(u * HEAD_DIM, (u + 1) * HEAD_DIM)
            kvc_ref[kg // NSA_GROUPS, 0, kg % NSA_GROUPS] = jnp.concatenate(
                [t[:, sl] for t in taps], axis=1)

    ksl = _rope_wide(proj(d + 2 * kvw, kvw), c, sp, sm)
    vsl = proj(d + 3 * kvw, kvw)
    kwn = _rope_wide(proj(d + 4 * kvw, kvw), c, sp, sm)
    vwn = proj(d + 5 * kvw, kvw)
    nblk = ks_ref.shape[3] - HEAD_DIM
    blk = (lax.broadcasted_iota(jnp.int32, (tm, nblk), 0) // SLC_BLOCK) + i * (tm // SLC_BLOCK)
    onehot = jnp.where(blk == lax.broadcasted_iota(jnp.int32, (tm, nblk), 1), 1.0, 0.0)
    vsl_t, vwn_t = vsl.T.astype(BF16), vwn.T.astype(BF16)
    ones_rows = _ones_rows(tm)
    for g in range(NSA_GROUPS):
        sl = slice(g * HEAD_DIM, (g + 1) * HEAD_DIM)
        ks_ref[0, g] = jnp.concatenate([ksl[:, sl], onehot], axis=1).astype(BF16)
        kw_ref[0, g] = kwn[:, sl].astype(BF16)
        vs_ref[0, g, 0, :HEAD_DIM, :] = vsl_t[sl]
        vs_ref[0, g, 0, HEAD_DIM:, :] = ones_rows
        for x in range(tm // WIN_TILE):
            vw_ref[0, g, x, :HEAD_DIM, :] = vwn_t[sl, x * WIN_TILE:(x + 1) * WIN_TILE]
            vw_ref[0, g, x, HEAD_DIM:, :] = ones_rows[:, :WIN_TILE]
    gate_ref[0] = jax.nn.sigmoid(proj(d + 6 * kvw, NSA_GROUPS * LANES))


def _nsa_proj(x, g, w_in, tables, tm):
    B, S, D = x.shape
    G, hd = NSA_GROUPS, HEAD_DIM
    kvw = G * hd
    nblk = S // SLC_BLOCK
    n_main = D + 6 * kvw
    wg = w_in[:, n_main:].reshape(D, G, NSA_HPG * 3)
    wg = jnp.pad(wg, ((0, 0), (0, 0), (0, LANES - NSA_HPG * 3))).reshape(D, G * LANES)
    w = jnp.concatenate([w_in[:, :n_main], wg], axis=1).astype(BF16)
    nw = w.shape[1]
    c, sp, sm = tables
    vrows = hd + ONES_ROWS
    wtiles = tm // WIN_TILE
    tab_spec = pl.BlockSpec((tm, LANES), lambda b, i: (i, 0))
    kv_spec = lambda width: pl.BlockSpec((1, G, tm, width), lambda b, i: (b, 0, i, 0))
    return pl.pallas_call(
        functools.partial(_nsa_proj_kernel, tm=tm, d=D),
        grid=(B, S // tm),
        in_specs=[
            pl.BlockSpec((1, tm, D), lambda b, i: (b, i, 0)),
            pl.BlockSpec((1, D), lambda b, i: (0, 0)),
            pl.BlockSpec((D, nw), lambda b, i: (0, 0)),
            tab_spec, tab_spec, tab_spec,
        ],
        out_specs=[
            pl.BlockSpec((1, tm, D), lambda b, i: (b, i, 0)),
            pl.BlockSpec((2, 1, G, tm // CMP_STRIDE, CMP_STRIDE * hd), lambda b, i: (0, b, 0, i, 0)),
            kv_spec(hd + nblk),
            pl.BlockSpec((1, G, 1, vrows, tm), lambda b, i: (b, 0, i, 0, 0)),
            kv_spec(hd),
            pl.BlockSpec((1, G, wtiles, vrows, WIN_TILE), lambda b, i: (b, 0, i, 0, 0)),
            pl.BlockSpec((1, tm, G * LANES), lambda b, i: (b, i, 0)),
        ],
        out_shape=[
            jax.ShapeDtypeStruct((B, S, D), BF16),
            jax.ShapeDtypeStruct((2, B, G, S // CMP_STRIDE, CMP_STRIDE * hd), BF16),
            jax.ShapeDtypeStruct((B, G, S, hd + nblk), BF16),
            jax.ShapeDtypeStruct((B, G, S // tm, vrows, tm), BF16),
            jax.ShapeDtypeStruct((B, G, S, hd), BF16),
            jax.ShapeDtypeStruct((B, G, S // WIN_TILE, vrows, WIN_TILE), BF16),
            jax.ShapeDtypeStruct((B, S, G * LANES), F32),
        ],
        scratch_shapes=[pltpu.VMEM((tm, LANES), F32)] * (2 * kvw // LANES),
        compiler_params=_params(("parallel", "parallel")),
        name="nsa_proj",
    )(x, g.reshape(1, D), w, c, sp, sm)


def _compress_kernel(r_ref, pa_ref, pb_ref, w1a_ref, w1b_ref, w2_ref, c_ref, sp_ref, sm_ref,
                     o_ref, ot_ref):
    r = r_ref[0, 0, 0]
    rows = r.shape[0]
    a = jnp.dot(r, w1a_ref[0], preferred_element_type=F32)
    b = jnp.dot(r, w1b_ref[0], preferred_element_type=F32)
    bias = (jnp.dot(pa_ref[0], w1a_ref[0], preferred_element_type=F32)
            + jnp.dot(pb_ref[0], w1b_ref[0], preferred_element_type=F32))
    hid = a + pltpu.roll(b, rows - 1, 0) + bias[0:1]
    hid = jax.nn.gelu(hid).astype(BF16)
    y = jnp.dot(hid, w2_ref[0], preferred_element_type=F32)
    y = _rope(y, c_ref[0], sp_ref[0], sm_ref[0])
    o_ref[0, 0, 0] = y[:, :HEAD_DIM].astype(BF16)
    ot_ref[0, 0, 0] = y.T[:HEAD_DIM].astype(BF16)


def _compress(r, pos, w1, w2):
    _, B, G, rows, rw = r.shape
    hd = HEAD_DIM
    w1 = w1.astype(BF16)
    hidden = w1.shape[2]
    posr = pos.reshape(2, 2, 1, rw).astype(BF16)
    pa = jnp.broadcast_to(posr[:, 0], (2, BF16_ROWS, rw))
    pb = jnp.broadcast_to(posr[:, 1], (2, BF16_ROWS, rw))
    w2p = jnp.pad(w2, ((0, 0), (0, 0), (0, LANES - hd))).astype(BF16)
    cmp_end = jnp.arange(rows) * CMP_STRIDE + CMP_BLOCK - 1
    c, sp, sm = _rope_tables(cmp_end)
    ident = (jnp.ones_like(c), jnp.zeros_like(sp), jnp.zeros_like(sm))
    c, sp, sm = (jnp.stack([t, u]) for t, u in zip((c, sp, sm), ident))
    per_kind = lambda shape: pl.BlockSpec((1,) + shape, lambda k, b, g: (k,) + (0,) * len(shape))
    return pl.pallas_call(
        _compress_kernel,
        grid=(2, B, G),
        in_specs=[
            pl.BlockSpec((1, 1, 1, rows, rw), lambda k, b, g: (k, b, g, 0, 0)),
            per_kind((BF16_ROWS, rw)), per_kind((BF16_ROWS, rw)),
            pl.BlockSpec((1, rw, hidden), lambda k, b, g: (k, 0, 0)),
            pl.BlockSpec((1, rw, hidden), lambda k, b, g: (k, 1, 0)),
            per_kind((hidden, LANES)),
            per_kind((rows, LANES)), per_kind((rows, LANES)), per_kind((rows, LANES)),
        ],
        out_specs=[pl.BlockSpec((1, 1, 1, rows, hd), lambda k, b, g: (k, b, g, 0, 0)),
                   pl.BlockSpec((1, 1, 1, hd, rows), lambda k, b, g: (k, b, g, 0, 0))],
        out_shape=[jax.ShapeDtypeStruct((2, B, G, rows, hd), BF16),
                   jax.ShapeDtypeStruct((2, B, G, hd, rows), BF16)],
        compiler_params=_params(("parallel", "parallel", "parallel")),
        name="nsa_compress",
    )(r, pa, pb, w1, w1, w2p, c, sp, sm)


def _causal_table(tq, tk):
    y = jnp.arange(2 * tk - tq)[:, None] - (tk - tq)
    return jnp.where(y <= jnp.arange(tq)[None, :], 0.0, NEG_INF).astype(F32)


def _tile_lanes(bias, s):
    tq = bias.shape[1]
    return jnp.concatenate(
        [s[:, x:x + tq] + bias for x in range(0, s.shape[1], tq)], axis=1)


def _flash_pair_scratch(chains, tk, r, vrows):
    return [pltpu.VMEM((chains, 2, tk, r), F32), pltpu.VMEM((chains, 2, tk, r), BF16),
            pltpu.VMEM((chains, 2, 1, r), F32), pltpu.VMEM((chains, 2, 1, r), F32),
            pltpu.VMEM((chains, 2, vrows, r), F32)]


def _flash_pair(qat_ref, k_at, vt_at, causal_ref, ia, ib, tq, tk, nsteps, scratch):
    s_ref, p_ref, al_ref, m_ref, acc_ref = scratch
    chains = range(qat_ref.shape[0])
    nfull = [(i * tq) // tk for i in (ia, ib)]

    def info(t):
        if isinstance(t, int) and t < 2:
            return t, nfull[t], True
        x = jnp.where(t - 2 >= nfull[0], 1, 0)
        return x, t - 2 - x * nfull[0], False

    def scores(t, slot):
        x, tile, diag = info(t)
        for c in chains:
            s = jnp.dot(k_at(c, tile), qat_ref[c, x], preferred_element_type=F32)
            if diag:
                i = (ia, ib)[x]
                start = pl.multiple_of((tk - tq) - (i * tq - nfull[x] * tk), tq)
                s = _tile_lanes(causal_ref[pl.ds(start, tk), :], s)
            s_ref[c, slot] = s

    def probs(t, slot):
        x, _, diag = info(t)
        for c in chains:
            m_new = jnp.max(s_ref[c, slot], axis=0, keepdims=True)
            if not diag:
                m = m_ref[c, x]
                m_new = jnp.maximum(m, m_new)
                al_ref[c, slot] = jnp.exp2(m - m_new)
            m_ref[c, x] = m_new
            p_ref[c, slot] = jnp.exp2(s_ref[c, slot] - m_new).astype(BF16)

    def accum(t, slot):
        x, tile, diag = info(t)
        for c in chains:
            pv = jnp.dot(vt_at(c, tile), p_ref[c, slot], preferred_element_type=F32)
            acc_ref[c, x] = pv if diag else al_ref[c, slot] * acc_ref[c, x] + pv

    def stage(t, a):
        scores(t + 2, a)
        probs(t + 1, 1 - a)
        accum(t, a)

    scores(0, 0)
    probs(0, 0)
    scores(1, 1)
    stage(0, 0)
    stage(1, 1)
    steady = nsteps - 4
    pairs = steady // 2

    def body(k, carry):
        stage(2 * k + 2, 0)
        stage(2 * k + 3, 1)
        return carry

    lax.fori_loop(0, pairs, body, 0)
    a = 0
    if steady % 2:
        stage(nsteps - 3, 0)
        a = 1
    accum(nsteps - 2, a)
    probs(nsteps - 1, 1 - a)
    accum(nsteps - 1, 1 - a)


def _block_penalty(imp, t0, tq, topk):
    ns = imp.shape[0]
    jblk = lax.broadcasted_iota(jnp.int32, (ns, tq), 0)
    tcol = t0 + lax.broadcasted_iota(jnp.int32, (ns, tq), 1)
    cur = tcol // SLC_BLOCK
    forced = (jblk == 0) | (jblk == cur) | (jblk == cur - 1)
    future = jblk * SLC_BLOCK > tcol
    val = jnp.where(forced, BIG, jnp.where(future, -BIG, imp))
    sub = SUBLANES
    vals =[val[a:a + sub] for a in range(0, ns, sub)]
    ranks = [jnp.zeros((sub, tq), F32) for _ in vals]
    row = lax.broadcasted_iota(jnp.int32, (sub, tq), 0)
    for i in range(ns):
        vi = jnp.broadcast_to(val[i:i + 1, :], (sub, tq))
        for a, va in enumerate(vals):
            if a * sub > i:
                hit = jnp.where(vi >= va, 1.0, 0.0)
            elif a * sub + sub - 1 < i:
                hit = jnp.where(vi > va, 1.0, 0.0)
            else:
                hit = jnp.where(row > i - a * sub, jnp.where(vi >= va, 1.0, 0.0),
                                jnp.where(vi > va, 1.0, 0.0))
            ranks[a] = ranks[a] + hit
    return jnp.concatenate([jnp.where(r < topk, 0.0, NEG_INF) for r in ranks], axis=0)


def _nsa_attn_kernel(qa_ref, qb_ref, ga_ref, gb_ref, kc_ref, vct_ref, selt_ref, ks_ref, vst_ref, kw_ref,
                     vwt_ref, cmask_ref, wmask_ref, causal_ref, o_ref, qat_ref, pre_ref, *scratch,
                     tq, tk, groups, topk):
    P, hd = NSA_HPG, HEAD_DIM
    R = P * tq
    qw = P * hd
    nq = o_ref.shape[1] // tq
    ia = pl.program_id(2)
    pair = ((0, ia, qa_ref, ga_ref), (1, nq - 1 - ia, qb_ref, gb_ref))
    selt = selt_ref[...]
    ns, ncp = selt.shape
    wk = WINDOW + tq
    gate_t = lambda g_ref, c: g_ref[0, :, c * LANES:(c + 1) * LANES].T
    cols = lambda p: slice(p * tq, (p + 1) * tq)

    for x, i, q_ref, g_ref in pair:
        t0 = i * tq
        wstart = pl.multiple_of(jnp.maximum(t0 - WINDOW, 0), tq)
        nck, nsb = (ncp // 2, ns // 2) if x == 0 else (ncp, ns)
        for c in range(groups):
            qt = q_ref[0, :, c * qw:(c + 1) * qw].astype(F32).T
            q_t = jnp.concatenate([qt[p * hd:(p + 1) * hd] for p in range(P)], axis=1).astype(BF16)
            gt = gate_t(g_ref, c)

            s = jnp.dot(kc_ref[0, 0, c, 0:nck, :], q_t, preferred_element_type=F32)
            s = _tile_lanes(cmask_ref[pl.ds(pl.multiple_of(ncp - t0 // CMP_STRIDE, 8), nck), :], s)
            m_c = jnp.max(s, axis=0, keepdims=True)
            p_c = jnp.exp2(s - m_c)
            l_c = jnp.sum(p_c, axis=0, keepdims=True)
            p_c = p_c * jnp.where(m_c > 0.5 * NEG_INF, 1.0 / l_c, 0.0)
            o_c = jnp.dot(vct_ref[0, 0, c, :, 0:nck], p_c.astype(BF16),
                          preferred_element_type=F32)

            rem = p_c[:, cols(0)]
            for p in range(1, P):
                rem = rem + p_c[:, cols(p)]
            imp = jnp.zeros((nsb, tq), F32)
            for _ in range(3):
                part = rem.astype(BF16)
                imp = imp + jnp.dot(selt[0:nsb, 0:nck], part, preferred_element_type=F32)
                rem = rem - part.astype(F32)
            pen = _block_penalty(imp, t0, tq, topk).astype(BF16)
            if nsb < ns:
                pen = jnp.concatenate([pen, jnp.zeros((ns - nsb, tq), BF16)], axis=0)
            qat_ref[c, x] = jnp.concatenate([q_t, jnp.concatenate([pen] * P, axis=1)], axis=0)

            s = jnp.dot(kw_ref[0, c, pl.ds(wstart, wk), :], q_t, preferred_element_type=F32)
            s = _tile_lanes(wmask_ref[pl.ds(pl.multiple_of(WINDOW - (t0 - wstart), tq), wk), :], s)
            p_w = jnp.exp2(s - jnp.max(s, axis=0, keepdims=True)).astype(BF16)
            wt0 = wstart // WIN_TILE
            acc_w = jnp.zeros((hd + ONES_ROWS, R), F32)
            for y in range(wk // WIN_TILE):
                acc_w = acc_w + jnp.dot(vwt_ref[0, c, wt0 + y], p_w[y * WIN_TILE:(y + 1) * WIN_TILE],
                                        preferred_element_type=F32)
            o_w = acc_w[:hd] / acc_w[hd:hd + 1]

            for p in range(P):
                pre_ref[c, x, p * hd:(p + 1) * hd, :] = (gt[3 * p:3 * p + 1] * o_c[:, cols(p)]
                                                         + gt[3 * p + 2:3 * p + 3] * o_w[:, cols(p)])

    _flash_pair(
        qat_ref,
        lambda c, j: ks_ref[0, c, pl.ds(pl.multiple_of(j * tk, tk), tk), :],
        lambda c, j: vst_ref[0, c, j],
        causal_ref, ia, nq - 1 - ia, tq, tk, (nq * tq) // tk + 1, scratch)

    acc_ref = scratch[-1]
    for x, i, _, g_ref in pair:
        for c in range(groups):
            acc = acc_ref[c, x]
            o_s = acc[:hd] / acc[hd:hd + 1]
            gt = gate_t(g_ref, c)
            out_t = jnp.concatenate(
                [pre_ref[c, x, p * hd:(p + 1) * hd, :] + gt[3 * p + 1:3 * p + 2] * o_s[:, cols(p)]
                 for p in range(P)], axis=0)
            o_ref[0, pl.ds(pl.multiple_of(i * tq, tq), tq), c * qw:(c + 1) * qw] = out_t.T.astype(BF16)


def _nsa_attn(q, gates, kvc_c, vct, ks, vst, kw, vwt, tq, tk):
    B, S, D = q.shape
    G, P, hd = NSA_GROUPS, NSA_HPG, HEAD_DIM
    groups = 2
    ncp = kvc_c.shape[3]
    ns = S // SLC_BLOCK
    nt = S // tk
    nwt = S // WIN_TILE
    topk = min(SLC_TOPK, ns)
    wk = WINDOW + tq
    tl = jnp.arange(tq)[None, :]
    y = jnp.arange(2 * ncp)[:, None] - ncp
    cmask = jnp.where(y * CMP_STRIDE + CMP_BLOCK - 1 <= tl, 0.0, NEG_INF).astype(F32)
    y = jnp.arange(WINDOW + wk)[:, None]
    wmask = jnp.where((y - WINDOW <= tl) & (y > tl), 0.0, NEG_INF).astype(F32)
    causal = _causal_table(tq, tk)
    whole = lambda a: pl.BlockSpec(a.shape, lambda b, g, i: (0,) * a.ndim)
    n = jnp.arange(ncp)
    c_start, c_end = n * CMP_STRIDE, n * CMP_STRIDE + CMP_BLOCK - 1
    sj = jnp.arange(ns) * SLC_BLOCK
    selt = ((c_end[None, :] >= sj[:, None]) & (c_start[None, :] <= sj[:, None] + SLC_BLOCK - 1)
            & (n[None, :] < ncp - 1)).astype(BF16)
    vrows = hd + ONES_ROWS
    nq = S // tq
    tile_a = lambda width: pl.BlockSpec((1, tq, groups * width), lambda b, g, i: (b, i, g))
    tile_b = lambda width: pl.BlockSpec((1, tq, groups * width), lambda b, g, i: (b, nq - 1 - i, g))
    return pl.pallas_call(
        functools.partial(_nsa_attn_kernel, tq=tq, tk=tk, groups=groups, topk=topk),
        grid=(B, G // groups, nq // 2),
        in_specs=[
            tile_a(P * hd), tile_b(P * hd),
            tile_a(LANES), tile_b(LANES),
            pl.BlockSpec((1, 1, groups, ncp, hd), lambda b, g, i: (0, b, g, 0, 0)),
            pl.BlockSpec((1, 1, groups, hd, ncp), lambda b, g, i: (1, b, g, 0, 0)),
            whole(selt),
            pl.BlockSpec((1, groups, S, hd + ns), lambda b, g, i: (b, g, 0, 0)),
            pl.BlockSpec((1, groups, nt, vrows, tk), lambda b, g, i: (b, g, 0, 0, 0)),
            pl.BlockSpec((1, groups, S, hd), lambda b, g, i: (b, g, 0, 0)),
            pl.BlockSpec((1, groups, nwt, vrows, WIN_TILE), lambda b, g, i: (b, g, 0, 0, 0)),
            whole(cmask), whole(wmask), whole(causal),
        ],
        out_specs=pl.BlockSpec((1, S, groups * P * hd), lambda b, g, i: (b, 0, g)),
        out_shape=jax.ShapeDtypeStruct((B, S, D), BF16),
        scratch_shapes=[pltpu.VMEM((groups, 2, hd + ns, P * tq), BF16),
                        pltpu.VMEM((groups, 2, P * hd, tq), F32)]
        + _flash_pair_scratch(groups, tk, P * tq, vrows),
        compiler_params=_params(("parallel", "parallel", "arbitrary")),
        name="nsa_attn",
    )(q, q, gates, gates, kvc_c, vct, selt, ks, vst, kw, vwt, cmask, wmask, causal)


def _diff_proj_kernel(x_ref, g_ref, w_ref, c_ref, sp_ref, sm_ref, q_ref, k_ref, vt_ref, *, d):
    h = _rms(x_ref[0], g_ref[...]).astype(BF16)
    c, sp, sm = c_ref[...], sp_ref[...], sm_ref[...]
    q = jnp.dot(h, w_ref[:, 0:d], preferred_element_type=F32)
    q_ref[0] = (_rope_wide(q, c, sp, sm) * (HEAD_DIM ** -0.5 * LOG2E)).astype(BF16)
    k = jnp.dot(h, w_ref[:, d:2 * d], preferred_element_type=F32)
    k_ref[0] = _rope_wide(k, c, sp, sm).astype(BF16)
    hw = 2 * HEAD_DIM
    v_t = jnp.dot(h, w_ref[:, 2 * d:3 * d], preferred_element_type=F32).T.astype(BF16)
    ones_rows = _ones_rows(v_t.shape[1])
    for hh in range(d // hw):
        vt_ref[0, hh, 0, :hw, :] = v_t[hh * hw:(hh + 1) * hw]
        vt_ref[0, hh, 0, hw:, :] = ones_rows


def _diff_proj(x, g, w_in, tables, tm):
    B, S, D = x.shape
    hw = 2 * HEAD_DIM
    H = D // hw
    c, sp, sm = tables
    tab_spec = pl.BlockSpec((tm, LANES), lambda b, i: (i, 0))
    tok_spec = pl.BlockSpec((1, tm, D), lambda b, i: (b, i, 0))
    return pl.pallas_call(
        functools.partial(_diff_proj_kernel, d=D),
        grid=(B, S // tm),
        in_specs=[tok_spec, pl.BlockSpec((1, D), lambda b, i: (0, 0)),
                  pl.BlockSpec((D, 3 * D), lambda b, i: (0, 0)), tab_spec, tab_spec, tab_spec],
        out_specs=[tok_spec, tok_spec,
                   pl.BlockSpec((1, H, 1, hw + ONES_ROWS, tm), lambda b, i: (b, 0, i, 0, 0))],
        out_shape=[jax.ShapeDtypeStruct((B, S, D), BF16), jax.ShapeDtypeStruct((B, S, D), BF16),
                   jax.ShapeDtypeStruct((B, H, S // tm, hw + ONES_ROWS, tm), BF16)],
        compiler_params=_params(("parallel", "parallel")),
        name="diff_proj",
    )(x, g.reshape(1, D), w_in.astype(BF16), c, sp, sm)


def _diff_attn_kernel(lq1_ref, lk1_ref, lq2_ref, lk2_ref, sg_ref, q_ref, k_ref, vt_ref, causal_ref,
                      o_ref, qat_ref, *scratch, tq, tk, heads, lambda_init):
    hw = 2 * HEAD_DIM
    nq = q_ref.shape[1] // tq
    ia = pl.program_id(2)
    ib = nq - 1 - ia
    pair = ((0, ia), (1, ib))
    rows = lambda i: pl.ds(pl.multiple_of(i * tq, tq), tq)
    lam = (jnp.exp(jnp.sum(lq1_ref[...] * lk1_ref[...], axis=1, keepdims=True))
           - jnp.exp(jnp.sum(lq2_ref[...] * lk2_ref[...], axis=1, keepdims=True)) + lambda_init)
    for c in range(heads):
        for x, i in pair:
            qt = q_ref[0, rows(i), c * hw:(c + 1) * hw].astype(F32).T
            row = lax.broadcasted_iota(jnp.int32, qt.shape, 0)
            qat_ref[c, x] = jnp.concatenate([jnp.where(row < HEAD_DIM, qt, 0.0),
                                             jnp.where(row >= HEAD_DIM, qt, 0.0)], axis=1).astype(BF16)
    _flash_pair(
        qat_ref,
        lambda c, j: k_ref[0, pl.ds(pl.multiple_of(j * tk, tk), tk), c * hw:(c + 1) * hw],
        lambda c, j: vt_ref[0, c, j],
        causal_ref, ia, ib, tq, tk, (nq * tq) // tk + 1, scratch)
    acc_ref = scratch[-1]
    for c in range(heads):
        for x, i in pair:
            acc = acc_ref[c, x]
            o = acc[:hw] / acc[hw:hw + 1]
            o = o[:, :tq] - lam * o[:, tq:]
            o = o * lax.rsqrt(jnp.mean(o * o, axis=0, keepdims=True) + EPS) * sg_ref[...]
            o_ref[0, rows(i), c * hw:(c + 1) * hw] = (o * (1.0 - lambda_init)).T.astype(BF16)


def _diff_attn(q, k, vt, lq1, lk1, lq2, lk2, subln_g, lambda_init, tq, tk, heads):
    B, S, D = q.shape
    hw = 2 * HEAD_DIM
    H = D // hw
    nt = S // tk
    vrows = hw + ONES_ROWS
    causal = _causal_table(tq, tk)
    vec = lambda a: a.reshape(1, -1).astype(F32)
    vec_spec = lambda n: pl.BlockSpec((1, n), lambda b, h, i: (0, 0))
    seq_spec = pl.BlockSpec((1, S, heads * hw), lambda b, h, i: (b, 0, h))
    return pl.pallas_call(
        functools.partial(_diff_attn_kernel, tq=tq, tk=tk, heads=heads, lambda_init=lambda_init),
        grid=(B, H // heads, S // tq // 2),
        in_specs=[vec_spec(HEAD_DIM)] * 4 + [
            pl.BlockSpec((hw, 1), lambda b, h, i: (0, 0)),
            seq_spec,
            seq_spec,
            pl.BlockSpec((1, heads, nt, vrows, tk), lambda b, h, i: (b, h, 0, 0, 0)),
            pl.BlockSpec(causal.shape, lambda b, h, i: (0, 0)),
        ],
        out_specs=seq_spec,
        out_shape=jax.ShapeDtypeStruct((B, S, D), BF16),
        scratch_shapes=[pltpu.VMEM((heads, 2, hw, 2 * tq), BF16)]
        + _flash_pair_scratch(heads, tk, 2 * tq, vrows),
        compiler_params=_params(("parallel", "parallel", "arbitrary")),
        name="diff_attn",
    )(vec(lq1), vec(lk1), vec(lq2), vec(lk2), subln_g.reshape(hw, 1).astype(F32), q, k, vt, causal)


def _post_kernel(x_ref, a_ref, wo_ref, g_ref, wu_ref, wd_ref, fg_ref, o_ref, *, ff_chunk, final):
    x = x_ref[...] + jnp.dot(a_ref[...], wo_ref[...], preferred_element_type=F32)
    h = _rms(x, g_ref[...]).astype(BF16)
    y = x
    for lo in range(0, wu_ref.shape[1], ff_chunk):
        u = jnp.maximum(jnp.dot(h, wu_ref[:, lo:lo + ff_chunk], preferred_element_type=F32), 0.0)
        y = y + jnp.dot((u * u).astype(BF16), wd_ref[lo:lo + ff_chunk, :], preferred_element_type=F32)
    if final:
        y = _rms(y, fg_ref[...])
    o_ref[...] = y


def _post(x, a, w_out, g, w_up, w_down, final_g, tm, ff_chunk):
    B, S, D = x.shape
    T = B * S
    dff = w_up.shape[1]
    final = final_g is not None
    fg = (final_g if final else jnp.ones((D,), F32)).reshape(1, D)
    const = lambda shape: pl.BlockSpec(shape, lambda i: (0, 0), pipeline_mode=pl.Buffered(1))
    tok = pl.BlockSpec((tm, D), lambda i: (i, 0))
    out = pl.pallas_call(
        functools.partial(_post_kernel, ff_chunk=ff_chunk, final=final),
        grid=(T // tm,),
        in_specs=[tok, tok, const((D, D)), const((1, D)), const((D, dff)), const((dff, D)),
                  const((1, D))],
        out_specs=tok,
        out_shape=jax.ShapeDtypeStruct((T, D), F32),
        compiler_params=_params(("parallel",)),
        name="post_mlp",
    )(x.reshape(T, D), a.reshape(T, D), w_out.astype(BF16), g.reshape(1, D),
      w_up.astype(BF16), w_down.astype(BF16), fg)
    return out.reshape(B, S, D)


def kernel(x, attn_norm_g, mlp_norm_g, nsa_w_in, nsa_ck_pos, nsa_ck_w1, nsa_ck_w2, nsa_cv_pos, nsa_cv_w1, nsa_cv_w2, nsa_w_out, diff_w_in, diff_lq1, diff_lk1, diff_lq2, diff_lk2, diff_subln_g, diff_w_out, mlp_w_up, mlp_w_down, final_norm_g):
    B, S, D = x.shape
    depth = attn_norm_g.shape[0]
    tables = _rope_tables(jnp.arange(S))
    tm = min(512, S)
    tm_mlp = min(1024, B * S)
    tk = tm
    for i in range(depth):
        j = i // 2
        if i % 2 == 0:
            q, kvc, ks, vst, kw, vwt, gates = _nsa_proj(x, attn_norm_g[i], nsa_w_in[j], tables, tm)
            kvc_c, vct = _compress(kvc,
                                   jnp.stack([nsa_ck_pos[j], nsa_cv_pos[j]]),
                                   jnp.stack([nsa_ck_w1[j], nsa_cv_w1[j]]),
                                   jnp.stack([nsa_ck_w2[j], nsa_cv_w2[j]]))
            a = _nsa_attn(q, gates, kvc_c, vct, ks, vst, kw, vwt, tq=256, tk=tk)
            w_out = nsa_w_out[j]
        else:
            lambda_init = 0.8 - 0.6 * math.exp(-0.3 * i)
            q, k, vt = _diff_proj(x, attn_norm_g[i], diff_w_in[j], tables, tm)
            a = _diff_attn(q, k, vt, diff_lq1[j], diff_lk1[j], diff_lq2[j], diff_lk2[j],
                           diff_subln_g[j], lambda_init, tq=tk, tk=tk, heads=2)
            w_out = diff_w_out[j]
        x = _post(x, a, w_out, mlp_norm_g[i], mlp_w_up[i], mlp_w_down[i],
                  final_norm_g if i == depth - 1 else None, tm_mlp, ff_chunk=1024)
    return x
```

```python
import functools
import math

import jax
import jax.numpy as jnp
from jax import lax
from jax.experimental import pallas as pl
from jax.experimental.pallas import tpu as pltpu

F32 = jnp.float32
BF16 = jnp.bfloat16

EPS = 1e-6
ROPE_THETA = 500000.0
HEAD_DIM = 64
ROT_HALF = HEAD_DIM // 8
LANES = 128
SUBLANES = 8
BF16_ROWS = 2 * SUBLANES
NEG_INF = -1e30
BIG = 1e9
LOG2E = math.log2(math.e)
ONES_ROWS = BF16_ROWS

NSA_GROUPS = 4
NSA_HPG = 4
CMP_BLOCK = 32
CMP_STRIDE = 16
SLC_BLOCK = 64
SLC_TOPK = 16
WINDOW = 512
WIN_TILE = 128

VMEM_LIMIT = 56 * 1024 * 1024

def _params(sem):
    return pltpu.CompilerParams(dimension_semantics=sem, vmem_limit_bytes=VMEM_LIMIT)


def _rms(x, g):
    return x * lax.rsqrt(jnp.mean(x * x, axis=-1, keepdims=True) + EPS) * g


def _rope_tables(positions):
    n = positions.shape[0]
    rot = 2 * ROT_HALF
    inv = 1.0 / (ROPE_THETA ** (jnp.arange(0, rot, 2, dtype=F32) / rot))
    ang = positions.astype(F32)[:, None] * inv[None, :]
    cos, sin = jnp.cos(ang), jnp.sin(ang)
    rest = HEAD_DIM - rot
    c = jnp.concatenate([cos, cos, jnp.ones((n, rest), F32)], axis=1)
    sp = jnp.concatenate([jnp.zeros((n, ROT_HALF), F32), sin, jnp.zeros((n, rest), F32)], axis=1)
    sm = jnp.concatenate([-sin, jnp.zeros((n, HEAD_DIM - ROT_HALF), F32)], axis=1)
    reps = LANES // HEAD_DIM
    return tuple(jnp.tile(t, (1, reps)) for t in (c, sp, sm))


def _rope(y, c, sp, sm):
    return (y * c + pltpu.roll(y, ROT_HALF, 1) * sp
            + pltpu.roll(y, LANES - ROT_HALF, 1) * sm)


def _rope_wide(y, c, sp, sm):
    n = y.shape[1] // LANES
    return jnp.concatenate(
        [_rope(y[:, k * LANES:(k + 1) * LANES], c, sp, sm) for k in range(n)], axis=1)


def _ones_rows(width):
    row = lax.broadcasted_iota(jnp.int32, (ONES_ROWS, width), 0)
    return jnp.where(row == 0, 1.0, 0.0).astype(BF16)


def _nsa_proj_kernel(x_ref, g_ref, w_ref, c_ref, sp_ref, sm_ref,
                     q_ref, kvc_ref, ks_ref, vs_ref, kw_ref, vw_ref, gate_ref, *stage_refs, tm, d):
    i = pl.program_id(1)
    h = _rms(x_ref[0], g_ref[...]).astype(BF16)
    c, sp, sm = c_ref[...], sp_ref[...], sm_ref[...]
    kvw = NSA_GROUPS * HEAD_DIM

    def proj(lo, width):
        return jnp.dot(h, w_ref[:, lo:lo + width], preferred_element_type=F32)

    q = proj(0, d)
    q_ref[0] = (_rope_wide(q, c, sp, sm) * (HEAD_DIM ** -0.5 * LOG2E)).astype(BF16)
    kvc = proj(d, 2 * kvw)
    per_chunk = LANES // HEAD_DIM
    for j, stage_ref in enumerate(stage_refs):
        stage_ref[...] = kvc[:, j * LANES:(j + 1) * LANES]
        taps = [stage_ref[pl.ds(l, tm // CMP_STRIDE, stride=CMP_STRIDE), :].astype(BF16)
                for l in range(CMP_STRIDE)]
        for u in range(per_chunk):
            kg = j * per_chunk + u
            sl = slice(u * HEAD_DIM, (u + 1) * HEAD_DIM)
            kvc_ref[kg // NSA_GROUPS, 0, kg % NSA_GROUPS] = jnp.concatenate(
                [t[:, sl] for t in taps], axis=1)

    ksl = _rope_wide(proj(d + 2 * kvw, kvw), c, sp, sm)
    vsl = proj(d + 3 * kvw, kvw)
    kwn = _rope_wide(proj(d + 4 * kvw, kvw), c, sp, sm)
    vwn = proj(d + 5 * kvw, kvw)
    nblk = ks_ref.shape[3] - HEAD_DIM
    blk = (lax.broadcasted_iota(jnp.int32, (tm, nblk), 0) // SLC_BLOCK) + i * (tm // SLC_BLOCK)
    onehot = jnp.where(blk == lax.broadcasted_iota(jnp.int32, (tm, nblk), 1), 1.0, 0.0)
    vsl_t, vwn_t = vsl.T.astype(BF16), vwn.T.astype(BF16)
    ones_rows = _ones_rows(tm)
    for g in range(NSA_GROUPS):
        sl = slice(g * HEAD_DIM, (g + 1) * HEAD_DIM)
        ks_ref[0, g] = jnp.concatenate([ksl[:, sl], onehot], axis=1).astype(BF16)
        kw_ref[0, g] = kwn[:, sl].astype(BF16)
        vs_ref[0, g, 0, :HEAD_DIM, :] = vsl_t[sl]
        vs_ref[0, g, 0, HEAD_DIM:, :] = ones_rows
        for x in range(tm // WIN_TILE):
            vw_ref[0, g, x, :HEAD_DIM, :] = vwn_t[sl, x * WIN_TILE:(x + 1) * WIN_TILE]
            vw_ref[0, g, x, HEAD_DIM:, :] = ones_rows[:, :WIN_TILE]
    gate_ref[0] = jax.nn.sigmoid(proj(d + 6 * kvw, NSA_GROUPS * LANES))


def _nsa_proj(x, g, w_in, tables, tm):
    B, S, D = x.shape
    G, hd = NSA_GROUPS, HEAD_DIM
    kvw = G * hd
    nblk = S // SLC_BLOCK
    n_main = D + 6 * kvw
    wg = w_in[:, n_main:].reshape(D, G, NSA_HPG * 3)
    wg = jnp.pad(wg, ((0, 0), (0, 0), (0, LANES - NSA_HPG * 3))).reshape(D, G * LANES)
    w = jnp.concatenate([w_in[:, :n_main], wg], axis=1).astype(BF16)
    nw = w.shape[1]
    c, sp, sm = tables
    vrows = hd + ONES_ROWS
    wtiles = tm // WIN_TILE
    tab_spec = pl.BlockSpec((tm, LANES), lambda b, i: (i, 0))
    kv_spec = lambda width: pl.BlockSpec((1, G, tm, width), lambda b, i: (b, 0, i, 0))
    return pl.pallas_call(
        functools.partial(_nsa_proj_kernel, tm=tm, d=D),
        grid=(B, S // tm),
        in_specs=[
            pl.BlockSpec((1, tm, D), lambda b, i: (b, i, 0)),
            pl.BlockSpec((1, D), lambda b, i: (0, 0)),
            pl.BlockSpec((D, nw), lambda b, i: (0, 0)),
            tab_spec, tab_spec, tab_spec,
        ],
        out_specs=[
            pl.BlockSpec((1, tm, D), lambda b, i: (b, i, 0)),
            pl.BlockSpec((2, 1, G, tm // CMP_STRIDE, CMP_STRIDE * hd), lambda b, i: (0, b, 0, i, 0)),
            kv_spec(hd + nblk),
            pl.BlockSpec((1, G, 1, vrows, tm), lambda b, i: (b, 0, i, 0, 0)),
            kv_spec(hd),
            pl.BlockSpec((1, G, wtiles, vrows, WIN_TILE), lambda b, i: (b, 0, i, 0, 0)),
            pl.BlockSpec((1, tm, G * LANES), lambda b, i: (b, i, 0)),
        ],
        out_shape=[
            jax.ShapeDtypeStruct((B, S, D), BF16),
            jax.ShapeDtypeStruct((2, B, G, S // CMP_STRIDE, CMP_STRIDE * hd), BF16),
            jax.ShapeDtypeStruct((B, G, S, hd + nblk), BF16),
            jax.ShapeDtypeStruct((B, G, S // tm, vrows, tm), BF16),
            jax.ShapeDtypeStruct((B, G, S, hd), BF16),
            jax.ShapeDtypeStruct((B, G, S // WIN_TILE, vrows, WIN_TILE), BF16),
            jax.ShapeDtypeStruct((B, S, G * LANES), F32),
        ],
        scratch_shapes=[pltpu.VMEM((tm, LANES), F32)] * (2 * kvw // LANES),
        compiler_params=_params(("parallel", "parallel")),
        name="nsa_proj",
    )(x, g.reshape(1, D), w, c, sp, sm)


def _compress_kernel(r_ref, pa_ref, pb_ref, w1a_ref, w1b_ref, w2_ref, c_ref, sp_ref, sm_ref,
                     o_ref, ot_ref):
    groups, rows = r_ref.shape[2], r_ref.shape[3]
    r = r_ref[0, 0].reshape(groups * rows, r_ref.shape[4])
    a = jnp.dot(r, w1a_ref[0], preferred_element_type=F32)
    b = jnp.dot(r, w1b_ref[0], preferred_element_type=F32)
    bias = (jnp.dot(pa_ref[0], w1a_ref[0], preferred_element_type=F32)
            + jnp.dot(pb_ref[0], w1b_ref[0], preferred_element_type=F32))
    hid = a + pltpu.roll(b, groups * rows - 1, 0) + bias[0:1]
    hid = jax.nn.gelu(hid).astype(BF16)
    y = jnp.dot(hid, w2_ref[0], preferred_element_type=F32)
    for g in range(groups):
        yg = _rope(y[g * rows:(g + 1) * rows], c_ref[0], sp_ref[0], sm_ref[0])
        o_ref[0, 0, g] = yg[:, :HEAD_DIM].astype(BF16)
        ot_ref[0, 0, g] = yg.T[:HEAD_DIM].astype(BF16)


def _compress(r, pos, w1, w2):
    _, B, G, rows, rw = r.shape
    hd = HEAD_DIM
    w1 = w1.astype(BF16)
    hidden = w1.shape[2]
    posr = pos.reshape(2, 2, 1, rw).astype(BF16)
    pa = jnp.broadcast_to(posr[:, 0], (2, BF16_ROWS, rw))
    pb = jnp.broadcast_to(posr[:, 1], (2, BF16_ROWS, rw))
    w2p = jnp.pad(w2, ((0, 0), (0, 0), (0, LANES - hd))).astype(BF16)
    cmp_end = jnp.arange(rows) * CMP_STRIDE + CMP_BLOCK - 1
    c, sp, sm = _rope_tables(cmp_end)
    ident = (jnp.ones_like(c), jnp.zeros_like(sp), jnp.zeros_like(sm))
    c, sp, sm = (jnp.stack([t, u]) for t, u in zip((c, sp, sm), ident))
    per_kind = lambda shape: pl.BlockSpec((1,) + shape, lambda k, b: (k,) + (0,) * len(shape))
    return pl.pallas_call(
        _compress_kernel,
        grid=(2, B),
        in_specs=[
            pl.BlockSpec((1, 1, G, rows, rw), lambda k, b: (k, b, 0, 0, 0)),
            per_kind((BF16_ROWS, rw)), per_kind((BF16_ROWS, rw)),
            pl.BlockSpec((1, rw, hidden), lambda k, b: (k, 0, 0)),
            pl.BlockSpec((1, rw, hidden), lambda k, b: (k, 1, 0)),
            per_kind((hidden, LANES)),
            per_kind((rows, LANES)), per_kind((rows, LANES)), per_kind((rows, LANES)),
        ],
        out_specs=[pl.BlockSpec((1, 1, G, rows, hd), lambda k, b: (k, b, 0, 0, 0)),
                   pl.BlockSpec((1, 1, G, hd, rows), lambda k, b: (k, b, 0, 0, 0))],
        out_shape=[jax.ShapeDtypeStruct((2, B, G, rows, hd), BF16),
                   jax.ShapeDtypeStruct((2, B, G, hd, rows), BF16)],
        compiler_params=_params(("parallel", "parallel")),
        name="nsa_compress",
    )(r, pa, pb, w1, w1, w2p, c, sp, sm)


def _causal_table(tq, tk):
    y = jnp.arange(2 * tk - tq)[:, None] - (tk - tq)
    return jnp.where(y <= jnp.arange(tq)[None, :], 0.0, NEG_INF).astype(F32)


def _tile_lanes(bias, s):
    tq = bias.shape[1]
    return jnp.concatenate(
        [s[:, x:x + tq] + bias for x in range(0, s.shape[1], tq)], axis=1)


def _flash_pair_scratch(chains, tk, r, vrows):
    return [pltpu.VMEM((chains, 2, tk, r), F32), pltpu.VMEM((chains, 2, tk, r), BF16),
            pltpu.VMEM((chains, 2, 1, r), F32), pltpu.VMEM((chains, 2, 1, r), F32),
            pltpu.VMEM((chains, 2, vrows, r), F32)]


def _flash_pair(qat_ref, k_at, vt_at, causal_ref, ia, ib, tq, tk, nsteps, scratch):
    s_ref, p_ref, al_ref, m_ref, acc_ref = scratch
    chains = range(qat_ref.shape[0])
    nfull = [(i * tq) // tk for i in (ia, ib)]

    def info(t):
        if isinstance(t, int) and t < 2:
            return t, nfull[t], True
        x = jnp.where(t - 2 >= nfull[0], 1, 0)
        return x, t - 2 - x * nfull[0], False

    def scores(t, slot):
        x, tile, diag = info(t)
        for c in chains:
            s = jnp.dot(k_at(c, tile), qat_ref[c, x], preferred_element_type=F32)
            if diag:
                i = (ia, ib)[x]
                start = pl.multiple_of((tk - tq) - (i * tq - nfull[x] * tk), tq)
                s = _tile_lanes(causal_ref[pl.ds(start, tk), :], s)
            s_ref[c, slot] = s

    def probs(t, slot):
        x, _, diag = info(t)
        for c in chains:
            m_new = jnp.max(s_ref[c, slot], axis=0, keepdims=True)
            if not diag:
                m = m_ref[c, x]
                m_new = jnp.maximum(m, m_new)
                al_ref[c, slot] = jnp.exp2(m - m_new)
            m_ref[c, x] = m_new
            p_ref[c, slot] = jnp.exp2(s_ref[c, slot] - m_new).astype(BF16)

    def accum(t, slot):
        x, tile, diag = info(t)
        for c in chains:
            pv = jnp.dot(vt_at(c, tile), p_ref[c, slot], preferred_element_type=F32)
            acc_ref[c, x] = pv if diag else al_ref[c, slot] * acc_ref[c, x] + pv

    def stage(t, a):
        scores(t + 2, a)
        probs(t + 1, 1 - a)
        accum(t, a)

    scores(0, 0)
    probs(0, 0)
    scores(1, 1)
    stage(0, 0)
    stage(1, 1)
    steady = nsteps - 4
    pairs = steady // 2

    def body(k, carry):
        stage(2 * k + 2, 0)
        stage(2 * k + 3, 1)
        return carry

    lax.fori_loop(0, pairs, body, 0)
    a = 0
    if steady % 2:
        stage(nsteps - 3, 0)
        a = 1
    accum(nsteps - 2, a)
    probs(nsteps - 1, 1 - a)
    accum(nsteps - 1, 1 - a)


def _block_penalty(imp, t0, tq, topk):
    ns = imp.shape[0]
    jblk = lax.broadcasted_iota(jnp.int32, (ns, tq), 0)
    tcol = t0 + lax.broadcasted_iota(jnp.int32, (ns, tq), 1)
    cur = tcol // SLC_BLOCK
    forced = (jblk == 0) | (jblk == cur) | (jblk == cur - 1)
    future = jblk * SLC_BLOCK > tcol
    val = jnp.where(forced, BIG, jnp.where(future, -BIG, imp))
    sub = SUBLANES
    vals =[val[a:a + sub] for a in range(0, ns, sub)]
    ranks = [jnp.zeros((sub, tq), F32) for _ in vals]
    row = lax.broadcasted_iota(jnp.int32, (sub, tq), 0)
    for i in range(ns):
        vi = jnp.broadcast_to(val[i:i + 1, :], (sub, tq))
        for a, va in enumerate(vals):
            if a * sub > i:
                hit = jnp.where(vi >= va, 1.0, 0.0)
            elif a * sub + sub - 1 < i:
                hit = jnp.where(vi > va, 1.0, 0.0)
            else:
                hit = jnp.where(row > i - a * sub, jnp.where(vi >= va, 1.0, 0.0),
                                jnp.where(vi > va, 1.0, 0.0))
            ranks[a] = ranks[a] + hit
    return jnp.concatenate([jnp.where(r < topk, 0.0, NEG_INF) for r in ranks], axis=0)


def _nsa_attn_kernel(qa_ref, qb_ref, ga_ref, gb_ref, kc_ref, vct_ref, selt_ref, ks_ref, vst_ref, kw_ref,
                     vwt_ref, cmask_ref, wmask_ref, causal_ref, o_ref, qat_ref, pre_ref, *scratch,
                     tq, tk, groups, topk):
    P, hd = NSA_HPG, HEAD_DIM
    R = P * tq
    qw = P * hd
    nq = o_ref.shape[1] // tq
    ia = pl.program_id(2)
    pair = ((0, ia, qa_ref, ga_ref), (1, nq - 1 - ia, qb_ref, gb_ref))
    selt = selt_ref[...]
    ns, ncp = selt.shape
    wk = WINDOW + tq
    gate_t = lambda g_ref, c: g_ref[0, :, c * LANES:(c + 1) * LANES].T
    cols = lambda p: slice(p * tq, (p + 1) * tq)

    for x, i, q_ref, g_ref in pair:
        t0 = i * tq
        wstart = pl.multiple_of(jnp.maximum(t0 - WINDOW, 0), tq)
        nck, nsb = (ncp // 2, ns // 2) if x == 0 else (ncp, ns)
        for c in range(groups):
            qt = q_ref[0, :, c * qw:(c + 1) * qw].astype(F32).T
            q_t = jnp.concatenate([qt[p * hd:(p + 1) * hd] for p in range(P)], axis=1).astype(BF16)
            gt = gate_t(g_ref, c)

            s = jnp.dot(kc_ref[0, 0, c, 0:nck, :], q_t, preferred_element_type=F32)
            s = _tile_lanes(cmask_ref[pl.ds(pl.multiple_of(ncp - t0 // CMP_STRIDE, 8), nck), :], s)
            m_c = jnp.max(s, axis=0, keepdims=True)
            p_c = jnp.exp2(s - m_c)
            l_c = jnp.sum(p_c, axis=0, keepdims=True)
            p_c = p_c * jnp.where(m_c > 0.5 * NEG_INF, 1.0 / l_c, 0.0)
            o_c = jnp.dot(vct_ref[0, 0, c, :, 0:nck], p_c.astype(BF16),
                          preferred_element_type=F32)

            rem = p_c[:, cols(0)]
            for p in range(1, P):
                rem = rem + p_c[:, cols(p)]
            imp = jnp.zeros((nsb, tq), F32)
            for _ in range(3):
                part = rem.astype(BF16)
                imp = imp + jnp.dot(selt[0:nsb, 0:nck], part, preferred_element_type=F32)
                rem = rem - part.astype(F32)
            pen = _block_penalty(imp, t0, tq, topk).astype(BF16)
            if nsb < ns:
                pen = jnp.concatenate([pen, jnp.zeros((ns - nsb, tq), BF16)], axis=0)
            qat_ref[c, x] = jnp.concatenate([q_t, jnp.concatenate([pen] * P, axis=1)], axis=0)

            s = jnp.dot(kw_ref[0, c, pl.ds(wstart, wk), :], q_t, preferred_element_type=F32)
            s = _tile_lanes(wmask_ref[pl.ds(pl.multiple_of(WINDOW - (t0 - wstart), tq), wk), :], s)
            p_w = jnp.exp2(s - jnp.max(s, axis=0, keepdims=True)).astype(BF16)
            wt0 = wstart // WIN_TILE
            acc_w = jnp.zeros((hd + ONES_ROWS, R), F32)
            for y in range(wk // WIN_TILE):
                acc_w = acc_w + jnp.dot(vwt_ref[0, c, wt0 + y], p_w[y * WIN_TILE:(y + 1) * WIN_TILE],
                                        preferred_element_type=F32)
            o_w = acc_w[:hd] / acc_w[hd:hd + 1]

            for p in range(P):
                pre_ref[c, x, p * hd:(p + 1) * hd, :] = (gt[3 * p:3 * p + 1] * o_c[:, cols(p)]
                                                         + gt[3 * p + 2:3 * p + 3] * o_w[:, cols(p)])

    _flash_pair(
        qat_ref,
        lambda c, j: ks_ref[0, c, pl.ds(pl.multiple_of(j * tk, tk), tk), :],
        lambda c, j: vst_ref[0, c, j],
        causal_ref, ia, nq - 1 - ia, tq, tk, (nq * tq) // tk + 1, scratch)

    acc_ref = scratch[-1]
    for x, i, _, g_ref in pair:
        for c in range(groups):
            acc = acc_ref[c, x]
            o_s = acc[:hd] / acc[hd:hd + 1]
            gt = gate_t(g_ref, c)
            out_t = jnp.concatenate(
                [pre_ref[c, x, p * hd:(p + 1) * hd, :] + gt[3 * p + 1:3 * p + 2] * o_s[:, cols(p)]
                 for p in range(P)], axis=0)
            o_ref[0, pl.ds(pl.multiple_of(i * tq, tq), tq), c * qw:(c + 1) * qw] = out_t.T.astype(BF16)


def _nsa_attn(q, gates, kvc_c, vct, ks, vst, kw, vwt, tq, tk):
    B, S, D = q.shape
    G, P, hd = NSA_GROUPS, NSA_HPG, HEAD_DIM
    groups = 2
    ncp = kvc_c.shape[3]
    ns = S // SLC_BLOCK
    nt = S // tk
    nwt = S // WIN_TILE
    topk = min(SLC_TOPK, ns)
    wk = WINDOW + tq
    tl = jnp.arange(tq)[None, :]
    y = jnp.arange(2 * ncp)[:, None] - ncp
    cmask = jnp.where(y * CMP_STRIDE + CMP_BLOCK - 1 <= tl, 0.0, NEG_INF).astype(F32)
    y = jnp.arange(WINDOW + wk)[:, None]
    wmask = jnp.where((y - WINDOW <= tl) & (y > tl), 0.0, NEG_INF).astype(F32)
    causal = _causal_table(tq, tk)
    whole = lambda a: pl.BlockSpec(a.shape, lambda b, g, i: (0,) * a.ndim)
    n = jnp.arange(ncp)
    c_start, c_end = n * CMP_STRIDE, n * CMP_STRIDE + CMP_BLOCK - 1
    sj = jnp.arange(ns) * SLC_BLOCK
    selt = ((c_end[None, :] >= sj[:, None]) & (c_start[None, :] <= sj[:, None] + SLC_BLOCK - 1)
            & (n[None, :] < ncp - 1)).astype(BF16)
    vrows = hd + ONES_ROWS
    nq = S // tq
    tile_a = lambda width: pl.BlockSpec((1, tq, groups * width), lambda b, g, i: (b, i, g))
    tile_b = lambda width: pl.BlockSpec((1, tq, groups * width), lambda b, g, i: (b, nq - 1 - i, g))
    return pl.pallas_call(
        functools.partial(_nsa_attn_kernel, tq=tq, tk=tk, groups=groups, topk=topk),
        grid=(B, G // groups, nq // 2),
        in_specs=[
            tile_a(P * hd), tile_b(P * hd),
            tile_a(LANES), tile_b(LANES),
            pl.BlockSpec((1, 1, groups, ncp, hd), lambda b, g, i: (0, b, g, 0, 0)),
            pl.BlockSpec((1, 1, groups, hd, ncp), lambda b, g, i: (1, b, g, 0, 0)),
            whole(selt),
            pl.BlockSpec((1, groups, S, hd + ns), lambda b, g, i: (b, g, 0, 0)),
            pl.BlockSpec((1, groups, nt, vrows, tk), lambda b, g, i: (b, g, 0, 0, 0)),
            pl.BlockSpec((1, groups, S, hd), lambda b, g, i: (b, g, 0, 0)),
            pl.BlockSpec((1, groups, nwt, vrows, WIN_TILE), lambda b, g, i: (b, g, 0, 0, 0)),
            whole(cmask), whole(wmask), whole(causal),
        ],
        out_specs=pl.BlockSpec((1, S, groups * P * hd), lambda b, g, i: (b, 0, g)),
        out_shape=jax.ShapeDtypeStruct((B, S, D), BF16),
        scratch_shapes=[pltpu.VMEM((groups, 2, hd + ns, P * tq), BF16),
                        pltpu.VMEM((groups, 2, P * hd, tq), F32)]
        + _flash_pair_scratch(groups, tk, P * tq, vrows),
        compiler_params=_params(("parallel", "parallel", "arbitrary")),
        name="nsa_attn",
    )(q, q, gates, gates, kvc_c, vct, selt, ks, vst, kw, vwt, cmask, wmask, causal)


def _diff_proj_kernel(x_ref, g_ref, w_ref, c_ref, sp_ref, sm_ref, q_ref, k_ref, vt_ref, *, d):
    h = _rms(x_ref[0], g_ref[...]).astype(BF16)
    c, sp, sm = c_ref[...], sp_ref[...], sm_ref[...]
    q = jnp.dot(h, w_ref[:, 0:d], preferred_element_type=F32)
    q_ref[0] = (_rope_wide(q, c, sp, sm) * (HEAD_DIM ** -0.5 * LOG2E)).astype(BF16)
    k = jnp.dot(h, w_ref[:, d:2 * d], preferred_element_type=F32)
    k_ref[0] = _rope_wide(k, c, sp, sm).astype(BF16)
    hw = 2 * HEAD_DIM
    v_t = jnp.dot(h, w_ref[:, 2 * d:3 * d], preferred_element_type=F32).T.astype(BF16)
    ones_rows = _ones_rows(v_t.shape[1])
    for hh in range(d // hw):
        vt_ref[0, hh, 0, :hw, :] = v_t[hh * hw:(hh + 1) * hw]
        vt_ref[0, hh, 0, hw:, :] = ones_rows


def _diff_proj(x, g, w_in, tables, tm):
    B, S, D = x.shape
    hw = 2 * HEAD_DIM
    H = D // hw
    c, sp, sm = tables
    tab_spec = pl.BlockSpec((tm, LANES), lambda b, i: (i, 0))
    tok_spec = pl.BlockSpec((1, tm, D), lambda b, i: (b, i, 0))
    return pl.pallas_call(
        functools.partial(_diff_proj_kernel, d=D),
        grid=(B, S // tm),
        in_specs=[tok_spec, pl.BlockSpec((1, D), lambda b, i: (0, 0)),
                  pl.BlockSpec((D, 3 * D), lambda b, i: (0, 0)), tab_spec, tab_spec, tab_spec],
        out_specs=[tok_spec, tok_spec,
                   pl.BlockSpec((1, H, 1, hw + ONES_ROWS, tm), lambda b, i: (b, 0, i, 0, 0))],
        out_shape=[jax.ShapeDtypeStruct((B, S, D), BF16), jax.ShapeDtypeStruct((B, S, D), BF16),
                   jax.ShapeDtypeStruct((B, H, S // tm, hw + ONES_ROWS, tm), BF16)],
        compiler_params=_params(("parallel", "parallel")),
        name="diff_proj",
    )(x, g.reshape(1, D), w_in.astype(BF16), c, sp, sm)


def _diff_attn_kernel(lq1_ref, lk1_ref, lq2_ref, lk2_ref, sg_ref, q_ref, k_ref, vt_ref, causal_ref,
                      o_ref, qat_ref, *scratch, tq, tk, heads, lambda_init):
    hw = 2 * HEAD_DIM
    nq = q_ref.shape[1] // tq
    ia = pl.program_id(2)
    ib = nq - 1 - ia
    pair = ((0, ia), (1, ib))
    rows = lambda i: pl.ds(pl.multiple_of(i * tq, tq), tq)
    lam = (jnp.exp(jnp.sum(lq1_ref[...] * lk1_ref[...], axis=1, keepdims=True))
           - jnp.exp(jnp.sum(lq2_ref[...] * lk2_ref[...], axis=1, keepdims=True)) + lambda_init)
    for c in range(heads):
        for x, i in pair:
            qt = q_ref[0, rows(i), c * hw:(c + 1) * hw].astype(F32).T
            row = lax.broadcasted_iota(jnp.int32, qt.shape, 0)
            qat_ref[c, x] = jnp.concatenate([jnp.where(row < HEAD_DIM, qt, 0.0),
                                             jnp.where(row >= HEAD_DIM, qt, 0.0)], axis=1).astype(BF16)
    _flash_pair(
        qat_ref,
        lambda c, j: k_ref[0, pl.ds(pl.multiple_of(j * tk, tk), tk), c * hw:(c + 1) * hw],
        lambda c, j: vt_ref[0, c, j],
        causal_ref, ia, ib, tq, tk, (nq * tq) // tk + 1, scratch)
    acc_ref = scratch[-1]
    for c in range(heads):
        for x, i in pair:
            acc = acc_ref[c, x]
            o = acc[:hw] / acc[hw:hw + 1]
            o = o[:, :tq] - lam * o[:, tq:]
            o = o * lax.rsqrt(jnp.mean(o * o, axis=0, keepdims=True) + EPS) * sg_ref[...]
            o_ref[0, rows(i), c * hw:(c + 1) * hw] = (o * (1.0 - lambda_init)).T.astype(BF16)


def _diff_attn(q, k, vt, lq1, lk1, lq2, lk2, subln_g, lambda_init, tq, tk, heads):
    B, S, D = q.shape
    hw = 2 * HEAD_DIM
    H = D // hw
    nt = S // tk
    vrows = hw + ONES_ROWS
    causal = _causal_table(tq, tk)
    vec = lambda a: a.reshape(1, -1).astype(F32)
    vec_spec = lambda n: pl.BlockSpec((1, n), lambda b, h, i: (0, 0))
    seq_spec = pl.BlockSpec((1, S, heads * hw), lambda b, h, i: (b, 0, h))
    return pl.pallas_call(
        functools.partial(_diff_attn_kernel, tq=tq, tk=tk, heads=heads, lambda_init=lambda_init),
        grid=(B, H // heads, S // tq // 2),
        in_specs=[vec_spec(HEAD_DIM)] * 4 + [
            pl.BlockSpec((hw, 1), lambda b, h, i: (0, 0)),
            seq_spec,
            seq_spec,
            pl.BlockSpec((1, heads, nt, vrows, tk), lambda b, h, i: (b, h, 0, 0, 0)),
            pl.BlockSpec(causal.shape, lambda b, h, i: (0, 0)),
        ],
        out_specs=seq_spec,
        out_shape=jax.ShapeDtypeStruct((B, S, D), BF16),
        scratch_shapes=[pltpu.VMEM((heads, 2, hw, 2 * tq), BF16)]
        + _flash_pair_scratch(heads, tk, 2 * tq, vrows),
        compiler_params=_params(("parallel", "parallel", "arbitrary")),
        name="diff_attn",
    )(vec(lq1), vec(lk1), vec(lq2), vec(lk2), subln_g.reshape(hw, 1).astype(F32), q, k, vt, causal)


def _post_kernel(x_ref, a_ref, wo_ref, g_ref, wu_ref, wd_ref, fg_ref, o_ref, *, ff_chunk, final):
    x = x_ref[...] + jnp.dot(a_ref[...], wo_ref[...], preferred_element_type=F32)
    h = _rms(x, g_ref[...]).astype(BF16)
    y = x
    for lo in range(0, wu_ref.shape[1], ff_chunk):
        u = jnp.maximum(jnp.dot(h, wu_ref[:, lo:lo + ff_chunk], preferred_element_type=F32), 0.0)
        y = y + jnp.dot((u * u).astype(BF16), wd_ref[lo:lo + ff_chunk, :], preferred_element_type=F32)
    if final:
        y = _rms(y, fg_ref[...])
    o_ref[...] = y


def _post(x, a, w_out, g, w_up, w_down, final_g, tm, ff_chunk):
    B, S, D = x.shape
    T = B * S
    dff = w_up.shape[1]
    final = final_g is not None
    fg = (final_g if final else jnp.ones((D,), F32)).reshape(1, D)
    const = lambda shape: pl.BlockSpec(shape, lambda i: (0, 0), pipeline_mode=pl.Buffered(1))
    tok = pl.BlockSpec((tm, D), lambda i: (i, 0))
    out = pl.pallas_call(
        functools.partial(_post_kernel, ff_chunk=ff_chunk, final=final),
        grid=(T // tm,),
        in_specs=[tok, tok, const((D, D)), const((1, D)), const((D, dff)), const((dff, D)),
                  const((1, D))],
        out_specs=tok,
        out_shape=jax.ShapeDtypeStruct((T, D), F32),
        compiler_params=_params(("parallel",)),
        name="post_mlp",
    )(x.reshape(T, D), a.reshape(T, D), w_out.astype(BF16), g.reshape(1, D),
      w_up.astype(BF16), w_down.astype(BF16), fg)
    return out.reshape(B, S, D)


def kernel(x, attn_norm_g, mlp_norm_g, nsa_w_in, nsa_ck_pos, nsa_ck_w1, nsa_ck_w2, nsa_cv_pos, nsa_cv_w1, nsa_cv_w2, nsa_w_out, diff_w_in, diff_lq1, diff_lk1, diff_lq2, diff_lk2, diff_subln_g, diff_w_out, mlp_w_up, mlp_w_down, final_norm_g):
    B, S, D = x.shape
    depth = attn_norm_g.shape[0]
    tables = _rope_tables(jnp.arange(S))
    tm = min(512, S)
    tm_mlp = min(1024, B * S)
    tk = tm
    for i in range(depth):
        j = i // 2
        if i % 2 == 0:
            q, kvc, ks, vst, kw, vwt, gates = _nsa_proj(x, attn_norm_g[i], nsa_w_in[j], tables, tm)
            kvc_c, vct = _compress(kvc,
                                   jnp.stack([nsa_ck_pos[j], nsa_cv_pos[j]]),
                                   jnp.stack([nsa_ck_w1[j], nsa_cv_w1[j]]),
                                   jnp.stack([nsa_ck_w2[j], nsa_cv_w2[j]]))
            a = _nsa_attn(q, gates, kvc_c, vct, ks, vst, kw, vwt, tq=256, tk=tk)
            w_out = nsa_w_out[j]
        else:
            lambda_init = 0.8 - 0.6 * math.exp(-0.3 * i)
            q, k, vt = _diff_proj(x, attn_norm_g[i], diff_w_in[j], tables, tm)
            a = _diff_attn(q, k, vt, diff_lq1[j], diff_lk1[j], diff_lq2[j], diff_lk2[j],
                           diff_subln_g[j], lambda_init, tq=tk, tk=tk, heads=2)
            w_out = diff_w_out[j]
        x = _post(x, a, w_out, mlp_norm_g[i], mlp_w_up[i], mlp_w_down[i],
                  final_norm_g if i == depth - 1 else None, tm_mlp, ff_chunk=1024)
    return x
```
